```python
import jax, jax.numpy as jnp
from jax import lax
import numpy as np

D_MODEL = 1024
BATCH = 16
SEQ = 2048
DEPTH = 1
DEC_BATCH = 32
DEC_SEQ = 16
PAST_LEN = 2048

CHUNK = 64
MIX_WIDTH = D_MODEL
HEAD_DIM = 64
ATT_WIDTH = D_MODEL // 2
ATT_HEADS = ATT_WIDTH // HEAD_DIM
CONV_CH = MIX_WIDTH - ATT_WIDTH
CONV_WIDTH = 31
IN_COLS = 3 * ATT_WIDTH + 2 * CONV_CH
Q_BLOCK = 128
PEER_HEADS = 8
PEER_KEY_DIM = 256
PEER_HALF = PEER_KEY_DIM // 2
N_KEYS = 128
N_EXPERTS = N_KEYS * N_KEYS
PEER_TOPK = 16
PEER_TOKEN_BLOCK = 256
EPS = 1e-6

kernel_name = "hybrid_stickbreak_conformer_peer_stream_step"


def _rmsnorm(x, g):
    x32 = x.astype(jnp.float32)
    y = x32 * lax.rsqrt(jnp.mean(x32 * x32, axis=-1, keepdims=True) + EPS)
    return (y * g.astype(jnp.float32)).astype(x.dtype)


def _layernorm(x, g, b):
    x32 = x.astype(jnp.float32)
    mu = jnp.mean(x32, axis=-1, keepdims=True)
    var = jnp.mean(jnp.square(x32 - mu), axis=-1, keepdims=True)
    y = (x32 - mu) * lax.rsqrt(var + EPS) * g.astype(jnp.float32) + b.astype(jnp.float32)
    return y.astype(x.dtype)


def _sb_block(q, k, v, q_pos, k_pos):
    scale = HEAD_DIM ** -0.5
    z = jnp.einsum('bhqd,bhkd->bhqk', q.astype(jnp.float32), k.astype(jnp.float32)) * scale
    mask = k_pos[None, :] < q_pos[:, None]
    log_beta = jax.nn.log_sigmoid(z)
    log_1m_beta = jnp.where(mask, jax.nn.log_sigmoid(-z), 0.0)
    after = lax.cumsum(log_1m_beta, axis=3, reverse=True) - log_1m_beta
    w = jnp.where(mask, jnp.exp(log_beta + after), 0.0)
    out = jnp.einsum('bhqk,bhkd->bhqd', w, v.astype(jnp.float32))
    return out.astype(v.dtype)


def _sb_prompt(q, k, v):
    B, H, S, Dh = q.shape
    nb = S // Q_BLOCK
    qb = q.reshape(B, H, nb, Q_BLOCK, Dh).transpose(2, 0, 1, 3, 4)
    pos = jnp.arange(S)
    qpos = pos.reshape(nb, Q_BLOCK)
    out = lax.map(lambda a: _sb_block(a[0], k, v, a[1], pos), (qb, qpos))
    return out.transpose(1, 2, 0, 3, 4).reshape(B, H, S, Dh)


def _dwconv(seq, w, b):
    out = lax.conv_general_dilated(seq, w[:, None, :].astype(seq.dtype), (1,), 'VALID',
                                   dimension_numbers=('NWC', 'WIO', 'NWC'),
                                   feature_group_count=CONV_CH)
    return out + b.astype(seq.dtype)


def _peer(h, peer_wq, peer_keys, peer_u, peer_v):
    B, T, D = h.shape
    tok = h.reshape(B * T, D)
    n = B * T
    pad = (-n) % PEER_TOKEN_BLOCK
    tok = jnp.pad(tok, ((0, pad), (0, 0)))
    blocks = tok.reshape(-1, PEER_TOKEN_BLOCK, D)

    def one(xb):
        qh = (xb @ peer_wq).reshape(PEER_TOKEN_BLOCK, PEER_HEADS, 2, PEER_HALF)
        s = jnp.einsum('thcd,ckd->thck', qh.astype(jnp.float32),
                       peer_keys.astype(jnp.float32))
        sv, si = lax.top_k(s, PEER_TOPK)
        cand = sv[:, :, 0, :, None] + sv[:, :, 1, None, :]
        cidx = si[:, :, 0, :, None] * N_KEYS + si[:, :, 1, None, :]
        cand = cand.reshape(PEER_TOKEN_BLOCK, PEER_HEADS, PEER_TOPK * PEER_TOPK)
        cidx = cidx.reshape(PEER_TOKEN_BLOCK, PEER_HEADS, PEER_TOPK * PEER_TOPK)
        top_s, pos = lax.top_k(cand, PEER_TOPK)
        eidx = jnp.take_along_axis(cidx, pos, axis=-1)
        g = jax.nn.softmax(top_s, axis=-1)
        u = jnp.take(peer_u, eidx, axis=0)
        act = jax.nn.gelu(jnp.einsum('thkd,td->thk', u, xb).astype(jnp.float32), approximate=False)
        coef = (g * act).astype(xb.dtype)
        vv = jnp.take(peer_v, eidx, axis=0)
        return jnp.einsum('thk,thkd->td', coef, vv)

    out = lax.map(one, blocks).reshape(-1, D)[:n]
    return out.reshape(B, T, D)


def _layer(x, past_k, past_v, conv_hist, norm1_g, w_in, q_norm_g, k_norm_g, conv_w, conv_b,
           conv_ln_g, conv_ln_b, w_out, norm2_g, peer_wq, peer_keys, peer_u, peer_v):
    B, T, _ = x.shape
    h = _rmsnorm(x, norm1_g)
    proj = h @ w_in
    q, k, v, a, gate = jnp.split(
        proj, [ATT_WIDTH, 2 * ATT_WIDTH, 3 * ATT_WIDTH, 3 * ATT_WIDTH + CONV_CH], axis=-1)

    def heads(t):
        return t.reshape(B, T, ATT_HEADS, HEAD_DIM).transpose(0, 2, 1, 3)

    q = heads(_rmsnorm(q.reshape(B, T, ATT_HEADS, HEAD_DIM), q_norm_g).reshape(B, T, ATT_WIDTH))
    k = heads(_rmsnorm(k.reshape(B, T, ATT_HEADS, HEAD_DIM), k_norm_g).reshape(B, T, ATT_WIDTH))
    v = heads(v)

    if past_k is None:
        att = _sb_prompt(q, k, v)
        conv_hist = jnp.zeros((B, CONV_WIDTH - 1, CONV_CH), x.dtype)
    else:
        P = past_k.shape[2]
        k_all = jnp.concatenate([past_k.astype(k.dtype), k], axis=2)
        v_all = jnp.concatenate([past_v.astype(v.dtype), v], axis=2)
        att = _sb_block(q, k_all, v_all, P + jnp.arange(T), jnp.arange(P + T))
    att = att.transpose(0, 2, 1, 3).reshape(B, T, ATT_WIDTH)

    glu = a * jax.nn.sigmoid(gate)
    conv_in = jnp.concatenate([conv_hist.astype(glu.dtype), glu], axis=1)
    c = _dwconv(conv_in, conv_w, conv_b)
    c = jax.nn.silu(_layernorm(c, conv_ln_g, conv_ln_b))
    new_conv = conv_in[:, -(CONV_WIDTH - 1):]

    x = x + jnp.concatenate([att, c], axis=-1) @ w_out
    x = x + _peer(_rmsnorm(x, norm2_g), peer_wq, peer_keys, peer_u, peer_v)
    return x, k, v, new_conv


def setup_inputs(seed: int = 0) -> dict:
    key = jax.random.key(seed)
    ks = jax.random.split(key, 20)
    f32 = jnp.float32
    nrm = lambda k, shape, s: jax.random.normal(k, shape, f32) * s
    return {
        "x_prompt": nrm(ks[0], (BATCH, SEQ, D_MODEL), 1.0),
        "x_sample": nrm(ks[1], (DEC_BATCH, DEC_SEQ, D_MODEL), 1.0),
        "cache_k": nrm(ks[2], (DEPTH, DEC_BATCH, ATT_HEADS, PAST_LEN, HEAD_DIM), 1.0),
        "cache_v": nrm(ks[3], (DEPTH, DEC_BATCH, ATT_HEADS, PAST_LEN, HEAD_DIM), 1.0),
        "state_conv": nrm(ks[4], (DEPTH, DEC_BATCH, CONV_WIDTH - 1, CONV_CH), 0.5),
        "norm1_g": 1.0 + nrm(ks[5], (DEPTH, D_MODEL), 0.02),
        "w_in": nrm(ks[6], (DEPTH, D_MODEL, IN_COLS), D_MODEL ** -0.5),
        "q_norm_g": 1.0 + nrm(ks[7], (DEPTH, HEAD_DIM), 0.02),
        "k_norm_g": 1.0 + nrm(ks[8], (DEPTH, HEAD_DIM), 0.02),
        "conv_w": nrm(ks[9], (DEPTH, CONV_WIDTH, CONV_CH), CONV_WIDTH ** -0.5),
        "conv_b": nrm(ks[10], (DEPTH, CONV_CH), 0.01),
        "conv_ln_g": 1.0 + nrm(ks[11], (DEPTH, CONV_CH), 0.02),
        "conv_ln_b": nrm(ks[12], (DEPTH, CONV_CH), 0.01),
        "w_out": nrm(ks[13], (DEPTH, MIX_WIDTH, D_MODEL), MIX_WIDTH ** -0.5),
        "norm2_g": 1.0 + nrm(ks[14], (DEPTH, D_MODEL), 0.02),
        "peer_wq": nrm(ks[15], (DEPTH, D_MODEL, PEER_HEADS * PEER_KEY_DIM), D_MODEL ** -0.5),
        "peer_keys": nrm(ks[16], (DEPTH, 2, N_KEYS, PEER_HALF), PEER_HALF ** -0.5),
        "peer_u": nrm(ks[17], (DEPTH, N_EXPERTS, D_MODEL), D_MODEL ** -0.5),
        "peer_v": nrm(ks[18], (DEPTH, N_EXPERTS, D_MODEL), 0.5),
    }


def reference(x_prompt, x_sample, cache_k, cache_v, state_conv, norm1_g, w_in, q_norm_g, k_norm_g,
              conv_w, conv_b, conv_ln_g, conv_ln_b, w_out, norm2_g, peer_wq, peer_keys, peer_u, peer_v):
    y_p, y_s = x_prompt, x_sample
    kp_list, vp_list, cp_list, ks_list, vs_list, cs_list = [], [], [], [], [], []
    for l in range(DEPTH):
        lw = (norm1_g[l], w_in[l], q_norm_g[l], k_norm_g[l], conv_w[l], conv_b[l],
              conv_ln_g[l], conv_ln_b[l], w_out[l], norm2_g[l], peer_wq[l], peer_keys[l],
              peer_u[l], peer_v[l])
        y_p, kp, vp, cp = _layer(y_p, None, None, None, *lw)
        y_s, ks_, vs_, cs_ = _layer(y_s, cache_k[l], cache_v[l], state_conv[l], *lw)
        kp_list.append(kp); vp_list.append(vp); cp_list.append(cp)
        ks_list.append(ks_); vs_list.append(vs_); cs_list.append(cs_)
    return (y_p, y_s, jnp.stack(kp_list), jnp.stack(vp_list), jnp.stack(cp_list),
            jnp.stack(ks_list), jnp.stack(vs_list), jnp.stack(cs_list))
```

```python
import functools
import math

import jax
import jax.numpy as jnp
from jax import lax
from jax.experimental import pallas as pl
from jax.experimental.pallas import tpu as pltpu

F32 = jnp.float32
BF16 = jnp.bfloat16

EPS = 1e-6
HEAD_DIM = 64
PEER_HEADS = 8
PEER_TOPK = 16
LANES = 128
SUBLANES = 8
ROW_TILE = 512
ATT_BLOCK = 128
PEER_TOKENS = 512
PEER_EXPERT_CHUNK = 512
PEER_SUB = 32
VMEM_LIMIT = 50 * 1024 * 1024
DEAD_LOG = -120.0

_NT = (((1,), (1,)), ((), ()))


def _const_spec(shape):
    zeros = (0,) * len(shape)
    return pl.BlockSpec(shape, lambda *_: zeros)


def _params(semantics):
    return pltpu.CompilerParams(dimension_semantics=semantics, vmem_limit_bytes=VMEM_LIMIT)


def _inproj_kernel(x_ref, g1_ref, w_ref, gq_ref, gk_ref, gm_ref,
                   q_ref, k_ref, v_ref, glu_ref, *, heads, att_w, conv_ch):
    x = x_ref[0]
    ms = jnp.mean(x * x, axis=-1, keepdims=True)
    hb = ((x * lax.rsqrt(ms + EPS)) * g1_ref[...]).astype(BF16)

    def proj(c0, width):
        return jnp.dot(hb, w_ref[:, c0:c0 + width], preferred_element_type=F32)

    def head_norm(t, g):
        tt = t * t
        hi = tt.astype(BF16)
        lo = (tt - hi.astype(F32)).astype(BF16)
        msq = (jnp.dot(hi, gm_ref[...], preferred_element_type=F32)
               + jnp.dot(lo, gm_ref[...], preferred_element_type=F32))
        return (t * lax.rsqrt(msq + EPS)) * g

    qn = head_norm(proj(0, att_w), gq_ref[...]) * (HEAD_DIM ** -0.5)
    kn = head_norm(proj(att_w, att_w), gk_ref[...])
    v = proj(2 * att_w, att_w)
    for h in range(heads):
        sl = slice(h * HEAD_DIM, (h + 1) * HEAD_DIM)
        q_ref[0, h] = qn[:, sl].astype(BF16)
        k_ref[0, h] = kn[:, sl]
        v_ref[0, h] = v[:, sl]
    a = proj(3 * att_w, conv_ch)
    gate = proj(3 * att_w + conv_ch, conv_ch)
    glu_ref[0] = a * jax.nn.sigmoid(gate)


def _inproj(x, g1, w_bf, gq, gk, gm):
    B, S, D = x.shape
    att_w = gq.shape[-1]
    heads = att_w // HEAD_DIM
    conv_ch = (w_bf.shape[1] - 3 * att_w) // 2
    T = min(S, ROW_TILE)
    assert S % T == 0
    hm = lambda dt: jax.ShapeDtypeStruct((B, heads, S, HEAD_DIM), dt)
    hspec = pl.BlockSpec((1, heads, T, HEAD_DIM), lambda b, s: (b, 0, s, 0))
    return pl.pallas_call(
        functools.partial(_inproj_kernel, heads=heads, att_w=att_w, conv_ch=conv_ch),
        grid=(B, S // T),
        in_specs=[pl.BlockSpec((1, T, D), lambda b, s: (b, s, 0)),
                  _const_spec(g1.shape), _const_spec(w_bf.shape),
                  _const_spec(gq.shape), _const_spec(gk.shape), _const_spec(gm.shape)],
        out_specs=[hspec, hspec, hspec, pl.BlockSpec((1, T, conv_ch), lambda b, s: (b, s, 0))],
        out_shape=[hm(BF16), hm(F32), hm(F32), jax.ShapeDtypeStruct((B, S, conv_ch), F32)],
        compiler_params=_params(("parallel", "parallel")),
        name="inproj",
    )(x, g1, w_bf, gq, gk, gm)


def _conv_kernel(glu_ref, hist_ref, w_ref, b_ref, lg_ref, lb_ref, c_ref, new_ref, ext_ref,
                 *, rows, width, chunk):
    hist = width - 1
    base = 32
    off = base - hist
    s = pl.program_id(1)

    @pl.when(s == 0)
    def _():
        ext_ref[off:base, :] = hist_ref[0]

    ext_ref[base:base + rows, :] = glu_ref[0]
    for r0 in range(0, rows, chunk):
        acc = ext_ref[off + r0:off + r0 + chunk, :] * w_ref[0:1, :]
        for w in range(1, width):
            acc = acc + ext_ref[off + r0 + w:off + r0 + w + chunk, :] * w_ref[w:w + 1, :]
        acc = acc + b_ref[...]
        mu = jnp.mean(acc, axis=-1, keepdims=True)
        d = acc - mu
        var = jnp.mean(d * d, axis=-1, keepdims=True)
        y = d * lax.rsqrt(var + EPS) * lg_ref[...] + lb_ref[...]
        c_ref[0, r0:r0 + chunk, :] = (y * jax.nn.sigmoid(y)).astype(BF16)
    tail = ext_ref[rows + off:rows + base, :]
    new_ref[0] = tail
    ext_ref[off:base, :] = tail


def _conv(glu, hist, cw, cb, lg, lb):
    B, S, C = glu.shape
    width = cw.shape[0]
    T = min(S, ROW_TILE)
    assert S % T == 0 and (T >= width - 1 or S == T)
    chunk = min(T, 32)
    return pl.pallas_call(
        functools.partial(_conv_kernel, rows=T, width=width, chunk=chunk),
        grid=(B, S // T),
        in_specs=[pl.BlockSpec((1, T, C), lambda b, s: (b, s, 0)),
                  pl.BlockSpec((1, width - 1, C), lambda b, s: (b, 0, 0)),
                  _const_spec(cw.shape), _const_spec(cb.shape),
                  _const_spec(lg.shape), _const_spec(lb.shape)],
        out_specs=[pl.BlockSpec((1, T, C), lambda b, s: (b, s, 0)),
                   pl.BlockSpec((1, width - 1, C), lambda b, s: (b, 0, 0))],
        out_shape=[jax.ShapeDtypeStruct((B, S, C), BF16),
                   jax.ShapeDtypeStruct((B, width - 1, C), F32)],
        scratch_shapes=[pltpu.VMEM((32 + T, C), F32)],
        compiler_params=_params(("parallel", "arbitrary")),
        name="conv",
    )(glu, hist, cw, cb, lg, lb)


def _sb_step(q, kb, vb, carry, upper, ones, mask):
    z = lax.dot_general(q, kb, _NT, preferred_element_type=F32)
    sp = jnp.log1p(jnp.exp(-jnp.abs(z)))
    log_beta = jnp.minimum(z, 0.0) - sp
    log_1m = log_beta - z
    if mask is not None:
        log_1m = jnp.where(mask, log_1m, 0.0)
    hi = log_1m.astype(BF16)
    lo = (log_1m - hi.astype(F32)).astype(BF16)
    after = (jnp.dot(hi, upper, preferred_element_type=F32)
             + jnp.dot(lo, upper, preferred_element_type=F32))
    total = (jnp.dot(hi, ones, preferred_element_type=F32)
             + jnp.dot(lo, ones, preferred_element_type=F32))
    w = jnp.exp(log_beta + after + carry[:, :1])
    if mask is not None:
        w = jnp.where(mask, w, 0.0)
    out = jnp.dot(w.astype(BF16), vb, preferred_element_type=F32)
    return out, carry + total


def _attn_kernel(*refs, heads, tq, tk, old_blocks, diag_in_old):
    if diag_in_old:
        q_ref, ko_ref, vo_ref, o_ref, acc_ref, carry_ref = refs
    else:
        q_ref, kd_ref, vd_ref, ko_ref, vo_ref, o_ref, acc_ref, carry_ref = refs
    qi = pl.program_id(1)

    def tri(n):
        r = lax.broadcasted_iota(jnp.int32, (n, n), 0)
        c = lax.broadcasted_iota(jnp.int32, (n, n), 1)
        return r, c

    r, c = tri(tq)
    mask = c < r
    upper_d = jnp.where(r > c, 1.0, 0.0).astype(BF16)
    ones_d = jnp.ones((tq, LANES), BF16)
    zero_carry = jnp.zeros((tq, LANES), F32)
    alive = None
    for h in range(heads):
        if diag_in_old:
            rows = pl.ds(pl.multiple_of(qi * tq, tq), tq)
            kb = ko_ref[0, h, rows, :].astype(BF16)
            vb = vo_ref[0, h, rows, :].astype(BF16)
        else:
            kb = kd_ref[0, h].astype(BF16)
            vb = vd_ref[0, h].astype(BF16)
        out, carry = _sb_step(q_ref[0, h], kb, vb, zero_carry, upper_d, ones_d, mask)
        acc_ref[h] = out
        carry_ref[h] = carry
        alive = carry if alive is None else jnp.maximum(alive, carry)

    if old_blocks is None:
        j0 = qi - 1
    else:
        j0 = jnp.int32(old_blocks - 1)

    r, c = tri(tk)
    upper_o = jnp.where(r > c, 1.0, 0.0).astype(BF16)
    ones_o = jnp.ones((tk, LANES), BF16)

    def cond(state):
        j, live = state
        return jnp.logical_and(j >= 0, live > DEAD_LOG)

    def body(state):
        j, _ = state
        rows = pl.ds(pl.multiple_of(j * tk, tk), tk)
        live = None
        for h in range(heads):
            kb = ko_ref[0, h, rows, :].astype(BF16)
            vb = vo_ref[0, h, rows, :].astype(BF16)
            out, carry = _sb_step(q_ref[0, h], kb, vb, carry_ref[h], upper_o, ones_o, None)
            acc_ref[h] = acc_ref[h] + out
            carry_ref[h] = carry
            live = carry if live is None else jnp.maximum(live, carry)
        return j - 1, jnp.max(live)

    lax.while_loop(cond, body, (j0, jnp.max(alive)))
    for h in range(heads):
        o_ref[0, :, h * HEAD_DIM:(h + 1) * HEAD_DIM] = acc_ref[h].astype(BF16)


def _attention(q, k_new, v_new, k_old=None, v_old=None):
    B, H, S, d = q.shape
    if k_old is None:
        tq = tk = min(S, ATT_BLOCK)
        assert S % tq == 0
        old_blocks, diag_in_old = None, True
        full = pl.BlockSpec((1, H, S, d), lambda b, i: (b, 0, 0, 0))
        in_specs = [pl.BlockSpec((1, H, tq, d), lambda b, i: (b, 0, i, 0)), full, full]
        args = (q, k_new, v_new)
    else:
        P = k_old.shape[2]
        tq, tk = S, min(P, ATT_BLOCK)
        assert P % tk == 0
        old_blocks, diag_in_old = P // tk, False
        new = pl.BlockSpec((1, H, S, d), lambda b, i: (b, 0, 0, 0))
        old = pl.BlockSpec((1, H, P, d), lambda b, i: (b, 0, 0, 0))
        in_specs = [new, new, new, old, old]
        args = (q, k_new, v_new, k_old, v_old)
    return pl.pallas_call(
        functools.partial(_attn_kernel, heads=H, tq=tq, tk=tk, old_blocks=old_blocks,
                          diag_in_old=diag_in_old),
        grid=(B, S // tq),
        in_specs=in_specs,
        out_specs=pl.BlockSpec((1, tq, H * d), lambda b, i: (b, i, 0)),
        out_shape=jax.ShapeDtypeStruct((B, S, H * d), BF16),
        scratch_shapes=[pltpu.VMEM((H, tq, d), F32), pltpu.VMEM((H, tq, LANES), F32)],
        compiler_params=_params(("parallel", "arbitrary")),
        name="attention",
    )(*args)


def _outproj_kernel(att_ref, c_ref, x_ref, wa_ref, wc_ref, g2_ref, x1_ref, xn_ref):
    x1 = (x_ref[...]
          + jnp.dot(att_ref[...], wa_ref[...], preferred_element_type=F32)
          + jnp.dot(c_ref[...], wc_ref[...], preferred_element_type=F32))
    x1_ref[...] = x1
    ms = jnp.mean(x1 * x1, axis=-1, keepdims=True)
    xn_ref[...] = ((x1 * lax.rsqrt(ms + EPS)) * g2_ref[...]).astype(BF16)


def _outproj(att, c, x, wa, wc, g2):
    N, D = x.shape
    T = min(N, ROW_TILE)
    assert N % T == 0
    row = lambda w: pl.BlockSpec((T, w), lambda i: (i, 0))
    return pl.pallas_call(
        _outproj_kernel,
        grid=(N // T,),
        in_specs=[row(att.shape[1]), row(c.shape[1]), row(D),
                  _const_spec(wa.shape), _const_spec(wc.shape), _const_spec(g2.shape)],
        out_specs=[row(D), row(D)],
        out_shape=[jax.ShapeDtypeStruct((N, D), F32), jax.ShapeDtypeStruct((N, D), BF16)],
        compiler_params=_params(("parallel",)),
        name="outproj",
    )(att, c, x, wa, wc, g2)


def _candidate_pairs():
    return [(p, q) for p in range(1, PEER_TOPK + 1) for q in range(1, PEER_TOPK + 1)
            if p * q <= PEER_TOPK]


def _top_values(load, count, dst_ref):
    def one(p, prev):
        accs = [None] * 4
        for n in range(count):
            v = load(n)
            v = jnp.where(v < prev, v, -jnp.inf)
            a = accs[n % 4]
            accs[n % 4] = v if a is None else jnp.maximum(a, v)
        m = jnp.maximum(jnp.maximum(accs[0], accs[1]), jnp.maximum(accs[2], accs[3]))
        dst_ref[p] = m
        return m
    shape = dst_ref.shape[1:]
    return lax.fori_loop(0, PEER_TOPK, one, jnp.full(shape, jnp.inf, F32))


def _peer_route(xn_ref, wq_ref, kb_ref, s_ref, top_ref, cand_ref, kth_ref,
                r0_ref, e0_ref, cnt_ref, e1_ref, *, nk, ph, tokens):
    xn = xn_ref[...]
    for c in range(2):
        qt = lax.dot_general(wq_ref[c], xn, _NT, preferred_element_type=F32)
        sc = jnp.dot(kb_ref[c], qt.astype(BF16), preferred_element_type=F32)
        for lt in range(tokens // LANES):
            s_ref[c, lt] = sc[:, lt * LANES:(lt + 1) * LANES]

    for c in range(2):
        _top_values(lambda n, c=c: s_ref[c, :, n * ph:(n + 1) * ph, :], nk, top_ref.at[c])

    pairs = _candidate_pairs()
    for n, (p, q) in enumerate(pairs):
        cand_ref[n] = top_ref[0, p - 1] + top_ref[1, q - 1]
    theta = _top_values(lambda n: cand_ref[n], len(pairs), kth_ref)
    a1 = top_ref[0, 0]
    b1 = top_ref[1, 0]
    best = a1 + b1
    denom = None
    for n in range(len(pairs)):
        cv = cand_ref[n]
        t = jnp.where(cv >= theta, jnp.exp(cv - best), 0.0)
        denom = t if denom is None else denom + t
    inv = 1.0 / denom

    def first_half(i, _):
        rows = pl.ds(pl.multiple_of(i * ph, ph), ph)
        s0 = s_ref[0, :, rows, :]
        rank = jnp.ones_like(s0)
        for p in range(PEER_TOPK):
            rank = rank + jnp.where(top_ref[0, p] > s0, 1.0, 0.0)
        r0_ref[:, rows, :] = rank
        e0_ref[:, rows, :] = jnp.exp(s0 - a1) * inv
        return 0
    lax.fori_loop(0, nk, first_half, 0)

    def second_half(j, _):
        rows = pl.ds(pl.multiple_of(j * ph, ph), ph)
        s1 = s_ref[1, :, rows, :]
        cnt = jnp.zeros_like(s1)
        for p in range(PEER_TOPK):
            cnt = cnt + jnp.where(top_ref[0, p] + s1 >= theta, 1.0, 0.0)
        s_ref[1, :, rows, :] = cnt
        s_ref[0, :, rows, :] = jnp.exp(s1 - b1)
        return 0
    lax.fori_loop(0, nk, second_half, 0)
    for lt in range(tokens // LANES):
        for h in range(ph):
            cnt_ref[lt, h * nk:(h + 1) * nk, :] = s_ref[1, lt, pl.ds(h, nk, stride=ph), :]
            e1_ref[lt, h * nk:(h + 1) * nk, :] = s_ref[0, lt, pl.ds(h, nk, stride=ph), :]


def _peer_kernel(xn_ref, x1_ref, wq_ref, kb_ref, u_ref, vt_ref, y_ref,
                 s_ref, top_ref, cand_ref, kth_ref, r0_ref, e0_ref, cnt_ref, e1_ref,
                 acc_ref, coef_ref, xu_ref, *, nk, ph, chunk, tokens):
    e = pl.program_id(1)

    @pl.when(e == 0)
    def _():
        _peer_route(xn_ref, wq_ref, kb_ref, s_ref, top_ref, cand_ref, kth_ref,
                    r0_ref, e0_ref, cnt_ref, e1_ref, nk=nk, ph=ph, tokens=tokens)
        acc_ref[...] = jnp.zeros_like(acc_ref)

    xu_ref[...] = lax.dot_general(u_ref[...], xn_ref[...], _NT, preferred_element_type=F32)
    blocks = chunk // nk

    def one_block(ib, _):
        i = e * blocks + ib
        hrows = pl.ds(pl.multiple_of(i * ph, ph), ph)
        for lt in range(tokens // LANES):
            lanes = slice(lt * LANES, (lt + 1) * LANES)
            rank = r0_ref[lt, hrows, :]
            e0 = e0_ref[lt, hrows, :]
            for jb in range(nk // PEER_SUB):
                w = None
                for h in range(ph):
                    jrows = slice(h * nk + jb * PEER_SUB, h * nk + (jb + 1) * PEER_SUB)
                    t = jnp.where(rank[h:h + 1] <= cnt_ref[lt, jrows, :],
                                  e1_ref[lt, jrows, :], 0.0) * e0[h:h + 1]
                    w = t if w is None else w + t
                erows = pl.ds(pl.multiple_of(ib * nk + jb * PEER_SUB, PEER_SUB), PEER_SUB)
                xu = xu_ref[erows, lanes]
                act = 0.5 * xu * (1.0 + lax.erf(xu * math.sqrt(0.5)))
                coef_ref[erows, lanes] = (w * act).astype(BF16)
        return 0
    lax.fori_loop(0, blocks, one_block, 0)
    acc_ref[...] += jnp.dot(vt_ref[...], coef_ref[...], preferred_element_type=F32)

    @pl.when(e == pl.num_programs(1) - 1)
    def _():
        y_ref[...] = x1_ref[...] + acc_ref[...].T


def _peer(xn, x1, wq_t, kbig, u_bf, vt_bf):
    N, D = xn.shape
    E = u_bf.shape[0]
    ph = PEER_HEADS
    nk = kbig.shape[1] // ph
    T = min(N, PEER_TOKENS)
    chunk = min(E, PEER_EXPERT_CHUNK)
    assert N % T == 0 and E % chunk == 0 and chunk % nk == 0 and T % LANES == 0
    npairs = len(_candidate_pairs())
    lt = T // LANES
    once = dict(pipeline_mode=pl.Buffered(1))
    return pl.pallas_call(
        functools.partial(_peer_kernel, nk=nk, ph=ph, chunk=chunk, tokens=T),
        grid=(N // T, E // chunk),
        in_specs=[pl.BlockSpec((T, D), lambda t, e: (t, 0)),
                  pl.BlockSpec((T, D), lambda t, e: (t, 0)),
                  pl.BlockSpec(wq_t.shape, lambda t, e: (0, 0, 0), **once),
                  pl.BlockSpec(kbig.shape, lambda t, e: (0, 0, 0), **once),
                  pl.BlockSpec((chunk, D), lambda t, e: (e, 0)),
                  pl.BlockSpec((D, chunk), lambda t, e: (0, e))],
        out_specs=pl.BlockSpec((T, D), lambda t, e: (t, 0)),
        out_shape=jax.ShapeDtypeStruct((N, D), F32),
        scratch_shapes=[pltpu.VMEM((2, lt, nk * ph, LANES), F32),
                        pltpu.VMEM((2, PEER_TOPK, lt, ph, LANES), F32),
                        pltpu.VMEM((npairs, lt, ph, LANES), F32),
                        pltpu.VMEM((PEER_TOPK, lt, ph, LANES), F32),
                        pltpu.VMEM((lt, nk * ph, LANES), F32),
                        pltpu.VMEM((lt, nk * ph, LANES), F32),
                        pltpu.VMEM((lt, ph * nk, LANES), F32),
                        pltpu.VMEM((lt, ph * nk, LANES), F32),
                        pltpu.VMEM((D, T), F32),
                        pltpu.VMEM((chunk, T), BF16),
                        pltpu.VMEM((chunk, T), F32)],
        compiler_params=_params(("parallel", "arbitrary")),
        name="peer",
    )(xn, x1, wq_t, kbig, u_bf, vt_bf)


def _layer_weights(norm1_g, w_in, q_norm_g, k_norm_g, conv_w, conv_b, conv_ln_g, conv_ln_b,
                   w_out, norm2_g, peer_wq, peer_keys, peer_u, peer_v):
    att_w = (w_in.shape[1] - 2 * conv_w.shape[1]) // 3
    heads = att_w // HEAD_DIM
    row = lambda v: v.reshape(1, -1)
    head_of = jnp.arange(att_w) // HEAD_DIM
    gm = jnp.where(head_of[:, None] == head_of[None, :], 1.0 / HEAD_DIM, 0.0).astype(BF16)
    _, nk, half = peer_keys.shape
    ph = PEER_HEADS
    D = peer_wq.shape[0]
    wq_t = peer_wq.reshape(D, ph, 2, half).transpose(2, 1, 3, 0).reshape(2, ph * half, D)
    eye = jnp.eye(ph, dtype=peer_keys.dtype)
    kbig = (peer_keys[:, :, None, None, :] * eye[None, None, :, :, None]
            ).reshape(2, nk * ph, ph * half)
    return dict(
        g1=row(norm1_g), w_in=w_in.astype(BF16),
        gq=row(jnp.tile(q_norm_g, heads)), gk=row(jnp.tile(k_norm_g, heads)), gm=gm,
        cw=conv_w, cb=row(conv_b), lg=row(conv_ln_g), lb=row(conv_ln_b),
        wa=w_out[:att_w].astype(BF16), wc=w_out[att_w:].astype(BF16), g2=row(norm2_g),
        wq_t=wq_t.astype(BF16), kbig=kbig.astype(BF16),
        u=peer_u.astype(BF16), vt=peer_v.T.astype(BF16))


def _layer(x, past_k, past_v, conv_hist, w):
    B, S, D = x.shape
    q, k, v, glu = _inproj(x, w["g1"], w["w_in"], w["gq"], w["gk"], w["gm"])
    if past_k is None:
        att = _attention(q, k, v)
        conv_hist = jnp.zeros((B, w["cw"].shape[0] - 1, glu.shape[-1]), x.dtype)
    else:
        att = _attention(q, k, v, past_k, past_v)
    c, new_conv = _conv(glu, conv_hist, w["cw"], w["cb"], w["lg"], w["lb"])
    x1, xn = _outproj(att.reshape(B * S, -1), c.reshape(B * S, -1), x.reshape(B * S, D),
                      w["wa"], w["wc"], w["g2"])
    y = _peer(xn, x1, w["wq_t"], w["kbig"], w["u"], w["vt"])
    return y.reshape(B, S, D), k, v, new_conv


def kernel(x_prompt, x_sample, cache_k, cache_v, state_conv, norm1_g, w_in, q_norm_g, k_norm_g,
           conv_w, conv_b, conv_ln_g, conv_ln_b, w_out, norm2_g, peer_wq, peer_keys, peer_u,
           peer_v):
    y_p, y_s = x_prompt, x_sample
    outs = [[] for _ in range(6)]
    for l in range(norm1_g.shape[0]):
        w = _layer_weights(norm1_g[l], w_in[l], q_norm_g[l], k_norm_g[l], conv_w[l], conv_b[l],
                           conv_ln_g[l], conv_ln_b[l], w_out[l], norm2_g[l], peer_wq[l],
                           peer_keys[l], peer_u[l], peer_v[l])
        y_p, kp, vp, cp = _layer(y_p, None, None, None, w)
        y_s, ks, vs, cs = _layer(y_s, cache_k[l], cache_v[l], state_conv[l], w)
        for lst, val in zip(outs, (kp, vp, cp, ks, vs, cs)):
            lst.append(val)
    return (y_p, y_s) + tuple(jnp.stack(o) for o in outs)
```

```python
import functools
import math

import jax
import jax.numpy as jnp
from jax import lax
from jax.experimental import pallas as pl
from jax.experimental.pallas import tpu as pltpu

F32 = jnp.float32
BF16 = jnp.bfloat16

EPS = 1e-6
HEAD_DIM = 64
PEER_HEADS = 8
PEER_TOPK = 16
LANES = 128
SUBLANES = 8
ROW_TILE = 512
ATT_BLOCK = 128
PEER_TOKENS = 512
PEER_EXPERT_CHUNK = 512
PEER_SUB = 16
VMEM_LIMIT = 50 * 1024 * 1024
DEAD_LOG = -120.0

_NT = (((1,), (1,)), ((), ()))


def _const_spec(shape):
    zeros = (0,) * len(shape)
    return pl.BlockSpec(shape, lambda *_: zeros)


def _params(semantics):
    return pltpu.CompilerParams(dimension_semantics=semantics, vmem_limit_bytes=VMEM_LIMIT)


def _inproj_kernel(x_ref, g1_ref, w_ref, gq_ref, gk_ref, gm_ref,
                   q_ref, k_ref, v_ref, glu_ref, *, heads, att_w, conv_ch):
    x = x_ref[0]
    ms = jnp.mean(x * x, axis=-1, keepdims=True)
    hb = ((x * lax.rsqrt(ms + EPS)) * g1_ref[...]).astype(BF16)

    def proj(c0, width):
        return jnp.dot(hb, w_ref[:, c0:c0 + width], preferred_element_type=F32)

    def head_norm(t, g):
        tt = t * t
        hi = tt.astype(BF16)
        lo = (tt - hi.astype(F32)).astype(BF16)
        msq = (jnp.dot(hi, gm_ref[...], preferred_element_type=F32)
               + jnp.dot(lo, gm_ref[...], preferred_element_type=F32))
        return (t * lax.rsqrt(msq + EPS)) * g

    qn = head_norm(proj(0, att_w), gq_ref[...]) * (HEAD_DIM ** -0.5)
    kn = head_norm(proj(att_w, att_w), gk_ref[...])
    v = proj(2 * att_w, att_w)
    for h in range(heads):
        sl = slice(h * HEAD_DIM, (h + 1) * HEAD_DIM)
        q_ref[0, h] = qn[:, sl].astype(BF16)
        k_ref[0, h] = kn[:, sl]
        v_ref[0, h] = v[:, sl]
    a = proj(3 * att_w, conv_ch)
    gate = proj(3 * att_w + conv_ch, conv_ch)
    glu_ref[0] = a * jax.nn.sigmoid(gate)


def _inproj(x, g1, w_bf, gq, gk, gm):
    B, S, D = x.shape
    att_w = gq.shape[-1]
    heads = att_w // HEAD_DIM
    conv_ch = (w_bf.shape[1] - 3 * att_w) // 2
    T = min(S, ROW_TILE)
    assert S % T == 0
    hm = lambda dt: jax.ShapeDtypeStruct((B, heads, S, HEAD_DIM), dt)
    hspec = pl.BlockSpec((1, heads, T, HEAD_DIM), lambda b, s: (b, 0, s, 0))
    return pl.pallas_call(
        functools.partial(_inproj_kernel, heads=heads, att_w=att_w, conv_ch=conv_ch),
        grid=(B, S // T),
        in_specs=[pl.BlockSpec((1, T, D), lambda b, s: (b, s, 0)),
                  _const_spec(g1.shape), _const_spec(w_bf.shape),
                  _const_spec(gq.shape), _const_spec(gk.shape), _const_spec(gm.shape)],
        out_specs=[hspec, hspec, hspec, pl.BlockSpec((1, T, conv_ch), lambda b, s: (b, s, 0))],
        out_shape=[hm(BF16), hm(F32), hm(F32), jax.ShapeDtypeStruct((B, S, conv_ch), F32)],
        compiler_params=_params(("parallel", "parallel")),
        name="inproj",
    )(x, g1, w_bf, gq, gk, gm)


def _conv_kernel(glu_ref, hist_ref, w_ref, b_ref, lg_ref, lb_ref, c_ref, new_ref, ext_ref,
                 *, rows, width, chunk):
    hist = width - 1
    base = 32
    off = base - hist
    s = pl.program_id(1)

    @pl.when(s == 0)
    def _():
        ext_ref[off:base, :] = hist_ref[0]

    ext_ref[base:base + rows, :] = glu_ref[0]
    for r0 in range(0, rows, chunk):
        acc = ext_ref[off + r0:off + r0 + chunk, :] * w_ref[0:1, :]
        for w in range(1, width):
            acc = acc + ext_ref[off + r0 + w:off + r0 + w + chunk, :] * w_ref[w:w + 1, :]
        acc = acc + b_ref[...]
        mu = jnp.mean(acc, axis=-1, keepdims=True)
        d = acc - mu
        var = jnp.mean(d * d, axis=-1, keepdims=True)
        y = d * lax.rsqrt(var + EPS) * lg_ref[...] + lb_ref[...]
        c_ref[0, r0:r0 + chunk, :] = (y * jax.nn.sigmoid(y)).astype(BF16)
    tail = ext_ref[rows + off:rows + base, :]
    new_ref[0] = tail
    ext_ref[off:base, :] = tail


def _conv(glu, hist, cw, cb, lg, lb):
    B, S, C = glu.shape
    width = cw.shape[0]
    T = min(S, ROW_TILE)
    assert S % T == 0 and (T >= width - 1 or S == T)
    chunk = min(T, 32)
    return pl.pallas_call(
        functools.partial(_conv_kernel, rows=T, width=width, chunk=chunk),
        grid=(B, S // T),
        in_specs=[pl.BlockSpec((1, T, C), lambda b, s: (b, s, 0)),
                  pl.BlockSpec((1, width - 1, C), lambda b, s: (b, 0, 0)),
                  _const_spec(cw.shape), _const_spec(cb.shape),
                  _const_spec(lg.shape), _const_spec(lb.shape)],
        out_specs=[pl.BlockSpec((1, T, C), lambda b, s: (b, s, 0)),
                   pl.BlockSpec((1, width - 1, C), lambda b, s: (b, 0, 0))],
        out_shape=[jax.ShapeDtypeStruct((B, S, C), BF16),
                   jax.ShapeDtypeStruct((B, width - 1, C), F32)],
        scratch_shapes=[pltpu.VMEM((32 + T, C), F32)],
        compiler_params=_params(("parallel", "arbitrary")),
        name="conv",
    )(glu, hist, cw, cb, lg, lb)


def _sb_step(q, kb, vb, carry, upper, ones, mask):
    z = lax.dot_general(q, kb, _NT, preferred_element_type=F32)
    sp = jnp.log1p(jnp.exp(-jnp.abs(z)))
    log_beta = jnp.minimum(z, 0.0) - sp
    log_1m = log_beta - z
    if mask is not None:
        log_1m = jnp.where(mask, log_1m, 0.0)
    hi = log_1m.astype(BF16)
    lo = (log_1m - hi.astype(F32)).astype(BF16)
    after = (jnp.dot(hi, upper, preferred_element_type=F32)
             + jnp.dot(lo, upper, preferred_element_type=F32))
    total = (jnp.dot(hi, ones, preferred_element_type=F32)
             + jnp.dot(lo, ones, preferred_element_type=F32))
    w = jnp.exp(log_beta + after + carry[:, :1])
    if mask is not None:
        w = jnp.where(mask, w, 0.0)
    out = jnp.dot(w.astype(BF16), vb, preferred_element_type=F32)
    return out, carry + total


def _attn_kernel(*refs, heads, tq, tk, old_blocks, diag_in_old):
    if diag_in_old:
        q_ref, ko_ref, vo_ref, o_ref, acc_ref, carry_ref = refs
    else:
        q_ref, kd_ref, vd_ref, ko_ref, vo_ref, o_ref, acc_ref, carry_ref = refs
    qi = pl.program_id(1)

    def tri(n):
        r = lax.broadcasted_iota(jnp.int32, (n, n), 0)
        c = lax.broadcasted_iota(jnp.int32, (n, n), 1)
        return r, c

    r, c = tri(tq)
    mask = c < r
    upper_d = jnp.where(r > c, 1.0, 0.0).astype(BF16)
    ones_d = jnp.ones((tq, LANES), BF16)
    zero_carry = jnp.zeros((tq, LANES), F32)
    alive = None
    for h in range(heads):
        if diag_in_old:
            rows = pl.ds(pl.multiple_of(qi * tq, tq), tq)
            kb = ko_ref[0, h, rows, :].astype(BF16)
            vb = vo_ref[0, h, rows, :].astype(BF16)
        else:
            kb = kd_ref[0, h].astype(BF16)
            vb = vd_ref[0, h].astype(BF16)
        out, carry = _sb_step(q_ref[0, h], kb, vb, zero_carry, upper_d, ones_d, mask)
        acc_ref[h] = out
        carry_ref[h] = carry
        alive = carry if alive is None else jnp.maximum(alive, carry)

    if old_blocks is None:
        j0 = qi - 1
    else:
        j0 = jnp.int32(old_blocks - 1)

    r, c = tri(tk)
    upper_o = jnp.where(r > c, 1.0, 0.0).astype(BF16)
    ones_o = jnp.ones((tk, LANES), BF16)

    def cond(state):
        j, live = state
        return jnp.logical_and(j >= 0, live > DEAD_LOG)

    def body(state):
        j, _ = state
        rows = pl.ds(pl.multiple_of(j * tk, tk), tk)
        live = None
        for h in range(heads):
            kb = ko_ref[0, h, rows, :].astype(BF16)
            vb = vo_ref[0, h, rows, :].astype(BF16)
            out, carry = _sb_step(q_ref[0, h], kb, vb, carry_ref[h], upper_o, ones_o, None)
            acc_ref[h] = acc_ref[h] + out
            carry_ref[h] = carry
            live = carry if live is None else jnp.maximum(live, carry)
        return j - 1, jnp.max(live)

    lax.while_loop(cond, body, (j0, jnp.max(alive)))
    for h in range(heads):
        o_ref[0, :, h * HEAD_DIM:(h + 1) * HEAD_DIM] = acc_ref[h].astype(BF16)


def _attention(q, k_new, v_new, k_old=None, v_old=None):
    B, H, S, d = q.shape
    if k_old is None:
        tq = tk = min(S, ATT_BLOCK)
        assert S % tq == 0
        old_blocks, diag_in_old = None, True
        full = pl.BlockSpec((1, H, S, d), lambda b, i: (b, 0, 0, 0))
        in_specs = [pl.BlockSpec((1, H, tq, d), lambda b, i: (b, 0, i, 0)), full, full]
        args = (q, k_new, v_new)
    else:
        P = k_old.shape[2]
        tq, tk = S, min(P, ATT_BLOCK)
        assert P % tk == 0
        old_blocks, diag_in_old = P // tk, False
        new = pl.BlockSpec((1, H, S, d), lambda b, i: (b, 0, 0, 0))
        old = pl.BlockSpec((1, H, P, d), lambda b, i: (b, 0, 0, 0))
        in_specs = [new, new, new, old, old]
        args = (q, k_new, v_new, k_old, v_old)
    return pl.pallas_call(
        functools.partial(_attn_kernel, heads=H, tq=tq, tk=tk, old_blocks=old_blocks,
                          diag_in_old=diag_in_old),
        grid=(B, S // tq),
        in_specs=in_specs,
        out_specs=pl.BlockSpec((1, tq, H * d), lambda b, i: (b, i, 0)),
        out_shape=jax.ShapeDtypeStruct((B, S, H * d), BF16),
        scratch_shapes=[pltpu.VMEM((H, tq, d), F32), pltpu.VMEM((H, tq, LANES), F32)],
        compiler_params=_params(("parallel", "arbitrary")),
        name="attention",
    )(*args)


def _outproj_kernel(att_ref, c_ref, x_ref, wa_ref, wc_ref, g2_ref, x1_ref, xn_ref):
    x1 = (x_ref[...]
          + jnp.dot(att_ref[...], wa_ref[...], preferred_element_type=F32)
          + jnp.dot(c_ref[...], wc_ref[...], preferred_element_type=F32))
    x1_ref[...] = x1
    ms = jnp.mean(x1 * x1, axis=-1, keepdims=True)
    xn_ref[...] = ((x1 * lax.rsqrt(ms + EPS)) * g2_ref[...]).astype(BF16)


def _outproj(att, c, x, wa, wc, g2):
    N, D = x.shape
    T = min(N, ROW_TILE)
    assert N % T == 0
    row = lambda w: pl.BlockSpec((T, w), lambda i: (i, 0))
    return pl.pallas_call(
        _outproj_kernel,
        grid=(N // T,),
        in_specs=[row(att.shape[1]), row(c.shape[1]), row(D),
                  _const_spec(wa.shape), _const_spec(wc.shape), _const_spec(g2.shape)],
        out_specs=[row(D), row(D)],
        out_shape=[jax.ShapeDtypeStruct((N, D), F32), jax.ShapeDtypeStruct((N, D), BF16)],
        compiler_params=_params(("parallel",)),
        name="outproj",
    )(att, c, x, wa, wc, g2)


def _candidate_pairs():
    return [(p, q) for p in range(1, PEER_TOPK + 1) for q in range(1, PEER_TOPK + 1)
            if p * q <= PEER_TOPK]


def _top_values(load, count, dst_ref):
    def one(p, prev):
        accs = [None] * 4
        for n in range(count):
            v = load(n)
            v = jnp.where(v < prev, v, -jnp.inf)
            a = accs[n % 4]
            accs[n % 4] = v if a is None else jnp.maximum(a, v)
        m = jnp.maximum(jnp.maximum(accs[0], accs[1]), jnp.maximum(accs[2], accs[3]))
        dst_ref[p] = m
        return m
    shape = dst_ref.shape[1:]
    return lax.fori_loop(0, PEER_TOPK, one, jnp.full(shape, jnp.inf, F32))


def _peer_route(xn_ref, wq_ref, kb_ref, s_ref, top_ref, cand_ref, kth_ref,
                r0_ref, e0_ref, cnt_ref, e1_ref, *, nk, ph, tokens):
    xn = xn_ref[...]
    for c in range(2):
        qt = lax.dot_general(wq_ref[c], xn, _NT, preferred_element_type=F32)
        sc = jnp.dot(kb_ref[c], qt.astype(BF16), preferred_element_type=F32)
        for lt in range(tokens // LANES):
            s_ref[c, lt] = sc[:, lt * LANES:(lt + 1) * LANES]

    for c in range(2):
        _top_values(lambda n, c=c: s_ref[c, :, n * ph:(n + 1) * ph, :], nk, top_ref.at[c])

    pairs = _candidate_pairs()
    for n, (p, q) in enumerate(pairs):
        cand_ref[n] = top_ref[0, p - 1] + top_ref[1, q - 1]
    theta = _top_values(lambda n: cand_ref[n], len(pairs), kth_ref)
    a1 = top_ref[0, 0]
    b1 = top_ref[1, 0]
    best = a1 + b1
    denom = None
    for n in range(len(pairs)):
        cv = cand_ref[n]
        t = jnp.where(cv >= theta, jnp.exp(cv - best), 0.0)
        denom = t if denom is None else denom + t
    inv = 1.0 / denom

    def first_half(i, _):
        rows = pl.ds(pl.multiple_of(i * ph, ph), ph)
        s0 = s_ref[0, :, rows, :]
        rank = jnp.ones_like(s0)
        for p in range(PEER_TOPK):
            rank = rank + jnp.where(top_ref[0, p] > s0, 1.0, 0.0)
        r0_ref[:, rows, :] = rank
        e0_ref[:, rows, :] = jnp.exp(s0 - a1) * inv
        return 0
    lax.fori_loop(0, nk, first_half, 0)

    def second_half(j, _):
        rows = pl.ds(pl.multiple_of(j * ph, ph), ph)
        s1 = s_ref[1, :, rows, :]
        cnt = jnp.zeros_like(s1)
        for p in range(PEER_TOPK):
            cnt = cnt + jnp.where(top_ref[0, p] + s1 >= theta, 1.0, 0.0)
        s_ref[1, :, rows, :] = cnt
        s_ref[0, :, rows, :] = jnp.exp(s1 - b1)
        return 0
    lax.fori_loop(0, nk, second_half, 0)
    for lt in range(tokens // LANES):
        for h in range(ph):
            cnt_ref[lt, h * nk:(h + 1) * nk, :] = (
                s_ref[1, lt, pl.ds(h, nk, stride=ph), :].astype(BF16))
            e1_ref[lt, h * nk:(h + 1) * nk, :] = (
                s_ref[0, lt, pl.ds(h, nk, stride=ph), :].astype(BF16))


def _peer_gate_chunk(c, xu_ref, coef_ref, r0_ref, e0_ref, cnt_ref, e1_ref, *, nk, ph, chunk, tokens):
    tile = (PEER_SUB, LANES)

    for ib in range(chunk // nk):
        i = c * (chunk // nk) + ib
        hrows = pl.ds(pl.multiple_of(i * ph, ph), ph)
        row0 = ib * nk
        for lt in range(tokens // LANES):
            lanes = slice(lt * LANES, (lt + 1) * LANES)
            rank = r0_ref[lt, hrows, :]
            e0 = e0_ref[lt, hrows, :]
            w = [None] * (nk // PEER_SUB)
            for h in range(ph):
                rank_b = jnp.broadcast_to(rank[h:h + 1], tile).astype(BF16)
                e0_b = jnp.broadcast_to(e0[h:h + 1], tile).astype(BF16)
                for jb in range(nk // PEER_SUB):
                    jrows = slice(h * nk + jb * PEER_SUB, h * nk + (jb + 1) * PEER_SUB)
                    t = jnp.where(rank_b <= cnt_ref[lt, jrows, :],
                                  e1_ref[lt, jrows, :], 0.0) * e0_b
                    w[jb] = t if w[jb] is None else w[jb] + t
            for jb in range(nk // PEER_SUB):
                erows = pl.ds(row0 + jb * PEER_SUB, PEER_SUB)
                xs = xu_ref[erows, lanes]
                act = 0.5 * xs * (1.0 + lax.erf(xs * math.sqrt(0.5)))
                coef_ref[erows, lanes] = w[jb] * act.astype(BF16)


def _peer_kernel(xn_ref, x1_ref, wq_ref, kb_ref, u_ref, vt_ref, y_ref,
                 s_ref, top_ref, cand_ref, kth_ref, r0_ref, e0_ref, cnt_ref, e1_ref,
                 acc_ref, xu_ref, coef_ref, *, nk, ph, chunk, tokens, n_chunks):
    e = pl.program_id(1)

    @pl.when(e == 0)
    def _():
        _peer_route(xn_ref, wq_ref, kb_ref, s_ref, top_ref, cand_ref, kth_ref,
                    r0_ref, e0_ref, cnt_ref, e1_ref, nk=nk, ph=ph, tokens=tokens)
        acc_ref[...] = jnp.zeros_like(acc_ref)
        xu_ref[1] = jnp.zeros(xu_ref.shape[1:], F32)
        coef_ref[...] = jnp.zeros_like(coef_ref)

    xn = xn_ref[...]
    gate = functools.partial(_peer_gate_chunk, r0_ref=r0_ref, e0_ref=e0_ref, cnt_ref=cnt_ref,
                             e1_ref=e1_ref, nk=nk, ph=ph, chunk=chunk, tokens=tokens)
    for phase in range(2):
        s_new, s_old = phase, 1 - phase
        rows = slice(phase * chunk, (phase + 1) * chunk)
        acc_ref[...] += jnp.dot(vt_ref[:, rows], coef_ref[s_new], preferred_element_type=F32)
        gate(jnp.clip(2 * e - 1 + phase, 0, n_chunks - 1), xu_ref.at[s_old], coef_ref.at[s_old])
        xu_ref[s_new] = lax.dot_general(u_ref[rows, :], xn, _NT, preferred_element_type=F32)

    @pl.when(e == pl.num_programs(1) - 1)
    def _():
        y_ref[...] = x1_ref[...] + acc_ref[...].T


def _peer(xn, x1, wq_t, kbig, u_bf, vt_bf):
    N, D = xn.shape
    E = u_bf.shape[0]
    ph = PEER_HEADS
    nk = kbig.shape[1] // ph
    T = min(N, PEER_TOKENS)
    chunk = min(E // 2, PEER_EXPERT_CHUNK)
    assert N % T == 0 and E % (2 * chunk) == 0 and chunk % nk == 0 and T % LANES == 0
    steps = E // (2 * chunk)
    npairs = len(_candidate_pairs())
    lt = T // LANES
    once = dict(pipeline_mode=pl.Buffered(1))
    return pl.pallas_call(
        functools.partial(_peer_kernel, nk=nk, ph=ph, chunk=chunk, tokens=T,
                          n_chunks=E // chunk),
        grid=(N // T, steps + 1),
        in_specs=[pl.BlockSpec((T, D), lambda t, e: (t, 0)),
                  pl.BlockSpec((T, D), lambda t, e: (t, 0)),
                  pl.BlockSpec(wq_t.shape, lambda t, e: (0, 0, 0), **once),
                  pl.BlockSpec(kbig.shape, lambda t, e: (0, 0, 0), **once),
                  pl.BlockSpec((2 * chunk, D), lambda t, e: (jnp.minimum(e, steps - 1), 0)),
                  pl.BlockSpec((D, 2 * chunk), lambda t, e: (0, jnp.maximum(e - 1, 0)))],
        out_specs=pl.BlockSpec((T, D), lambda t, e: (t, 0)),
        out_shape=jax.ShapeDtypeStruct((N, D), F32),
        scratch_shapes=[pltpu.VMEM((2, lt, nk * ph, LANES), F32),
                        pltpu.VMEM((2, PEER_TOPK, lt, ph, LANES), F32),
                        pltpu.VMEM((npairs, lt, ph, LANES), F32),
                        pltpu.VMEM((PEER_TOPK, lt, ph, LANES), F32),
                        pltpu.VMEM((lt, nk * ph, LANES), F32),
                        pltpu.VMEM((lt, nk * ph, LANES), F32),
                        pltpu.VMEM((lt, ph * nk, LANES), BF16),
                        pltpu.VMEM((lt, ph * nk, LANES), BF16),
                        pltpu.VMEM((D, T), F32),
                        pltpu.VMEM((2, chunk, T), F32),
                        pltpu.VMEM((2, chunk, T), BF16)],
        compiler_params=_params(("parallel", "arbitrary")),
        name="peer",
    )(xn, x1, wq_t, kbig, u_bf, vt_bf)


def _layer_weights(norm1_g, w_in, q_norm_g, k_norm_g, conv_w, conv_b, conv_ln_g, conv_ln_b,
                   w_out, norm2_g, peer_wq, peer_keys, peer_u, peer_v):
    att_w = (w_in.shape[1] - 2 * conv_w.shape[1]) // 3
    heads = att_w // HEAD_DIM
    row = lambda v: v.reshape(1, -1)
    head_of = jnp.arange(att_w) // HEAD_DIM
    gm = jnp.where(head_of[:, None] == head_of[None, :], 1.0 / HEAD_DIM, 0.0).astype(BF16)
    _, nk, half = peer_keys.shape
    ph = PEER_HEADS
    D = peer_wq.shape[0]
    wq_t = peer_wq.reshape(D, ph, 2, half).transpose(2, 1, 3, 0).reshape(2, ph * half, D)
    eye = jnp.eye(ph, dtype=peer_keys.dtype)
    kbig = (peer_keys[:, :, None, None, :] * eye[None, None, :, :, None]
            ).reshape(2, nk * ph, ph * half)
    return dict(
        g1=row(norm1_g), w_in=w_in.astype(BF16),
        gq=row(jnp.tile(q_norm_g, heads)), gk=row(jnp.tile(k_norm_g, heads)), gm=gm,
        cw=conv_w, cb=row(conv_b), lg=row(conv_ln_g), lb=row(conv_ln_b),
        wa=w_out[:att_w].astype(BF16), wc=w_out[att_w:].astype(BF16), g2=row(norm2_g),
        wq_t=wq_t.astype(BF16), kbig=kbig.astype(BF16),
        u=peer_u.astype(BF16), vt=peer_v.T.astype(BF16))


def _layer(x, past_k, past_v, conv_hist, w):
    B, S, D = x.shape
    q, k, v, glu = _inproj(x, w["g1"], w["w_in"], w["gq"], w["gk"], w["gm"])
    if past_k is None:
        att = _attention(q, k, v)
        conv_hist = jnp.zeros((B, w["cw"].shape[0] - 1, glu.shape[-1]), x.dtype)
    else:
        att = _attention(q, k, v, past_k, past_v)
    c, new_conv = _conv(glu, conv_hist, w["cw"], w["cb"], w["lg"], w["lb"])
    x1, xn = _outproj(att.reshape(B * S, -1), c.reshape(B * S, -1), x.reshape(B * S, D),
                      w["wa"], w["wc"], w["g2"])
    y = _peer(xn, x1, w["wq_t"], w["kbig"], w["u"], w["vt"])
    return y.reshape(B, S, D), k, v, new_conv


def kernel(x_prompt, x_sample, cache_k, cache_v, state_conv, norm1_g, w_in, q_norm_g, k_norm_g,
           conv_w, conv_b, conv_ln_g, conv_ln_b, w_out, norm2_g, peer_wq, peer_keys, peer_u,
           peer_v):
    y_p, y_s = x_prompt, x_sample
    outs = [[] for _ in range(6)]
    for l in range(norm1_g.shape[0]):
        w = _layer_weights(norm1_g[l], w_in[l], q_norm_g[l], k_norm_g[l], conv_w[l], conv_b[l],
                           conv_ln_g[l], conv_ln_b[l], w_out[l], norm2_g[l], peer_wq[l],
                           peer_keys[l], peer_u[l], peer_v[l])
        y_p, kp, vp, cp = _layer(y_p, None, None, None, w)
        y_s, ks, vs, cs = _layer(y_s, cache_k[l], cache_v[l], state_conv[l], w)
        for lst, val in zip(outs, (kp, vp, cp, ks, vs, cs)):
            lst.append(val)
    return (y_p, y_s) + tuple(jnp.stack(o) for o in outs)
```

```python
import functools
import math

import jax
import jax.numpy as jnp
from jax import lax
from jax.experimental import pallas as pl
from jax.experimental.pallas import tpu as pltpu

F32 = jnp.float32
BF16 = jnp.bfloat16

EPS = 1e-6
HEAD_DIM = 64
PEER_HEADS = 8
PEER_TOPK = 16
LANES = 128
SUBLANES = 8
ROW_TILE = 512
ATT_BLOCK = 128
PEER_TOKENS = 512
PEER_EXPERT_CHUNK = 1024
PEER_SUB = 16
VMEM_LIMIT = 60 * 1024 * 1024
DEAD_LOG = -120.0

_NT = (((1,), (1,)), ((), ()))


def _const_spec(shape):
    zeros = (0,) * len(shape)
    return pl.BlockSpec(shape, lambda *_: zeros)


def _params(semantics):
    return pltpu.CompilerParams(dimension_semantics=semantics, vmem_limit_bytes=VMEM_LIMIT)


def _inproj_kernel(x_ref, g1_ref, w_ref, gq_ref, gk_ref, gm_ref,
                   q_ref, k_ref, v_ref, glu_ref, *, heads, att_w, conv_ch):
    x = x_ref[0]
    ms = jnp.mean(x * x, axis=-1, keepdims=True)
    hb = ((x * lax.rsqrt(ms + EPS)) * g1_ref[...]).astype(BF16)

    def proj(c0, width):
        return jnp.dot(hb, w_ref[:, c0:c0 + width], preferred_element_type=F32)

    def head_norm(t, g):
        tt = t * t
        hi = tt.astype(BF16)
        lo = (tt - hi.astype(F32)).astype(BF16)
        msq = (jnp.dot(hi, gm_ref[...], preferred_element_type=F32)
               + jnp.dot(lo, gm_ref[...], preferred_element_type=F32))
        return (t * lax.rsqrt(msq + EPS)) * g

    qn = head_norm(proj(0, att_w), gq_ref[...]) * (HEAD_DIM ** -0.5)
    kn = head_norm(proj(att_w, att_w), gk_ref[...])
    v = proj(2 * att_w, att_w)
    for h in range(heads):
        sl = slice(h * HEAD_DIM, (h + 1) * HEAD_DIM)
        q_ref[0, h] = qn[:, sl].astype(BF16)
        k_ref[0, h] = kn[:, sl]
        v_ref[0, h] = v[:, sl]
    a = proj(3 * att_w, conv_ch)
    gate = proj(3 * att_w + conv_ch, conv_ch)
    glu_ref[0] = a * jax.nn.sigmoid(gate)


def _inproj(x, g1, w_bf, gq, gk, gm):
    B, S, D = x.shape
    att_w = gq.shape[-1]
    heads = att_w // HEAD_DIM
    conv_ch = (w_bf.shape[1] - 3 * att_w) // 2
    T = min(S, ROW_TILE)
    assert S % T == 0
    hm = lambda dt: jax.ShapeDtypeStruct((B, heads, S, HEAD_DIM), dt)
    hspec = pl.BlockSpec((1, heads, T, HEAD_DIM), lambda b, s: (b, 0, s, 0))
    return pl.pallas_call(
        functools.partial(_inproj_kernel, heads=heads, att_w=att_w, conv_ch=conv_ch),
        grid=(B, S // T),
        in_specs=[pl.BlockSpec((1, T, D), lambda b, s: (b, s, 0)),
                  _const_spec(g1.shape), _const_spec(w_bf.shape),
                  _const_spec(gq.shape), _const_spec(gk.shape), _const_spec(gm.shape)],
        out_specs=[hspec, hspec, hspec, pl.BlockSpec((1, T, conv_ch), lambda b, s: (b, s, 0))],
        out_shape=[hm(BF16), hm(F32), hm(F32), jax.ShapeDtypeStruct((B, S, conv_ch), F32)],
        compiler_params=_params(("parallel", "parallel")),
        name="inproj",
    )(x, g1, w_bf, gq, gk, gm)


def _conv_kernel(glu_ref, hist_ref, w_ref, b_ref, lg_ref, lb_ref, c_ref, new_ref, ext_ref,
                 *, rows, width, chunk):
    hist = width - 1
    base = 32
    off = base - hist
    s = pl.program_id(1)

    @pl.when(s == 0)
    def _():
        ext_ref[off:base, :] = hist_ref[0]

    ext_ref[base:base + rows, :] = glu_ref[0]
    for r0 in range(0, rows, chunk):
        acc = ext_ref[off + r0:off + r0 + chunk, :] * w_ref[0:1, :]
        for w in range(1, width):
            acc = acc + ext_ref[off + r0 + w:off + r0 + w + chunk, :] * w_ref[w:w + 1, :]
        acc = acc + b_ref[...]
        mu = jnp.mean(acc, axis=-1, keepdims=True)
        d = acc - mu
        var = jnp.mean(d * d, axis=-1, keepdims=True)
        y = d * lax.rsqrt(var + EPS) * lg_ref[...] + lb_ref[...]
        c_ref[0, r0:r0 + chunk, :] = (y * jax.nn.sigmoid(y)).astype(BF16)
    tail = ext_ref[rows + off:rows + base, :]
    new_ref[0] = tail
    ext_ref[off:base, :] = tail


def _conv(glu, hist, cw, cb, lg, lb):
    B, S, C = glu.shape
    width = cw.shape[0]
    T = min(S, ROW_TILE)
    assert S % T == 0 and (T >= width - 1 or S == T)
    chunk = min(T, 32)
    return pl.pallas_call(
        functools.partial(_conv_kernel, rows=T, width=width, chunk=chunk),
        grid=(B, S // T),
        in_specs=[pl.BlockSpec((1, T, C), lambda b, s: (b, s, 0)),
                  pl.BlockSpec((1, width - 1, C), lambda b, s: (b, 0, 0)),
                  _const_spec(cw.shape), _const_spec(cb.shape),
                  _const_spec(lg.shape), _const_spec(lb.shape)],
        out_specs=[pl.BlockSpec((1, T, C), lambda b, s: (b, s, 0)),
                   pl.BlockSpec((1, width - 1, C), lambda b, s: (b, 0, 0))],
        out_shape=[jax.ShapeDtypeStruct((B, S, C), BF16),
                   jax.ShapeDtypeStruct((B, width - 1, C), F32)],
        scratch_shapes=[pltpu.VMEM((32 + T, C), F32)],
        compiler_params=_params(("parallel", "arbitrary")),
        name="conv",
    )(glu, hist, cw, cb, lg, lb)


def _sb_block(qs, kbs, vbs, carries, upper, ones, mask):
    tq, tk = qs[0].shape[0], kbs[0].shape[0]
    zs = [lax.dot_general(q, kb, _NT, preferred_element_type=F32) for q, kb in zip(qs, kbs)]
    log_betas, sums = [], []
    for z in zs:
        sp = jnp.log1p(jnp.exp(-jnp.abs(z)))
        log_beta = jnp.minimum(z, 0.0) - sp
        log_1m = log_beta - z
        if mask is not None:
            log_1m = jnp.where(mask, log_1m, 0.0)
        hi = log_1m.astype(BF16)
        lo = (log_1m - hi.astype(F32)).astype(BF16)
        both = jnp.concatenate([hi, lo], axis=0)
        if tk == LANES:
            p = jnp.dot(both, jnp.concatenate([upper, ones], axis=1),
                        preferred_element_type=F32)
            after, total = p[:, :tk], p[:, tk:]
        else:
            after = jnp.dot(both, upper, preferred_element_type=F32)
            total = jnp.dot(both, ones, preferred_element_type=F32)
        log_betas.append(log_beta)
        sums.append((after[:tq] + after[tq:], total[:tq] + total[tq:]))
    outs, new_carries = [], []
    for log_beta, (after, total), carry, vb in zip(log_betas, sums, carries, vbs):
        logw = log_beta + after
        if carry is not None:
            logw = logw + (carry if tk == LANES else carry[:, :1])
        w = jnp.exp(logw)
        if mask is not None:
            w = jnp.where(mask, w, 0.0)
        outs.append(jnp.dot(w.astype(BF16), vb, preferred_element_type=F32))
        new_carries.append(total if carry is None else carry + total)
    return outs, new_carries


def _max_all(xs):
    m = xs[0]
    for x in xs[1:]:
        m = jnp.maximum(m, x)
    return jnp.max(m)


def _attn_kernel(*refs, heads, tq, tk, old_blocks, diag_in_old):
    if diag_in_old:
        q_ref, ko_ref, vo_ref, o_ref, acc_ref, carry_ref = refs
    else:
        q_ref, kd_ref, vd_ref, ko_ref, vo_ref, o_ref, acc_ref, carry_ref = refs
    qi = pl.program_id(1)

    def tri(n):
        r = lax.broadcasted_iota(jnp.int32, (n, n), 0)
        c = lax.broadcasted_iota(jnp.int32, (n, n), 1)
        return r, c

    r, c = tri(tq)
    mask = c < r
    upper_d = jnp.where(r > c, 1.0, 0.0).astype(BF16)
    ones_d = jnp.ones((tq, LANES), BF16)
    hs = range(heads)
    qs = [q_ref[0, h] for h in hs]
    if diag_in_old:
        rows = pl.ds(pl.multiple_of(qi * tq, tq), tq)
        kbs = [ko_ref[0, h, rows, :].astype(BF16) for h in hs]
        vbs = [vo_ref[0, h, rows, :].astype(BF16) for h in hs]
    else:
        kbs = [kd_ref[0, h].astype(BF16) for h in hs]
        vbs = [vd_ref[0, h].astype(BF16) for h in hs]
    outs, carries = _sb_block(qs, kbs, vbs, [None] * heads, upper_d, ones_d, mask)
    for h in hs:
        acc_ref[h] = outs[h]
        carry_ref[h] = carries[h]

    if old_blocks is None:
        j0 = qi - 1
    else:
        j0 = jnp.int32(old_blocks - 1)

    r, c = tri(tk)
    upper_o = jnp.where(r > c, 1.0, 0.0).astype(BF16)
    ones_o = jnp.ones((tk, LANES), BF16)

    def cond(state):
        j, live = state
        return jnp.logical_and(j >= 0, live > DEAD_LOG)

    def body(state):
        j, _ = state
        rows = pl.ds(pl.multiple_of(j * tk, tk), tk)
        qs = [q_ref[0, h] for h in hs]
        kbs = [ko_ref[0, h, rows, :].astype(BF16) for h in hs]
        vbs = [vo_ref[0, h, rows, :].astype(BF16) for h in hs]
        old = [carry_ref[h] for h in hs]
        accs = [acc_ref[h] for h in hs]
        outs, new = _sb_block(qs, kbs, vbs, old, upper_o, ones_o, None)
        for h in hs:
            acc_ref[h] = accs[h] + outs[h]
            carry_ref[h] = new[h]
        return j - 1, _max_all(new)

    lax.while_loop(cond, body, (j0, _max_all(carries)))
    for h in range(heads):
        o_ref[0, :, h * HEAD_DIM:(h + 1) * HEAD_DIM] = acc_ref[h].astype(BF16)


def _attention(q, k_new, v_new, k_old=None, v_old=None):
    B, H, S, d = q.shape
    if k_old is None:
        tq = tk = min(S, ATT_BLOCK)
        assert S % tq == 0
        old_blocks, diag_in_old = None, True
        full = pl.BlockSpec((1, H, S, d), lambda b, i: (b, 0, 0, 0))
        in_specs = [pl.BlockSpec((1, H, tq, d), lambda b, i: (b, 0, i, 0)), full, full]
        args = (q, k_new, v_new)
    else:
        P = k_old.shape[2]
        tq, tk = S, min(P, ATT_BLOCK)
        assert P % tk == 0
        old_blocks, diag_in_old = P // tk, False
        new = pl.BlockSpec((1, H, S, d), lambda b, i: (b, 0, 0, 0))
        old = pl.BlockSpec((1, H, P, d), lambda b, i: (b, 0, 0, 0))
        in_specs = [new, new, new, old, old]
        args = (q, k_new, v_new, k_old, v_old)
    return pl.pallas_call(
        functools.partial(_attn_kernel, heads=H, tq=tq, tk=tk, old_blocks=old_blocks,
                          diag_in_old=diag_in_old),
        grid=(B, S // tq),
        in_specs=in_specs,
        out_specs=pl.BlockSpec((1, tq, H * d), lambda b, i: (b, i, 0)),
        out_shape=jax.ShapeDtypeStruct((B, S, H * d), BF16),
        scratch_shapes=[pltpu.VMEM((H, tq, d), F32), pltpu.VMEM((H, tq, LANES), F32)],
        compiler_params=_params(("parallel", "arbitrary")),
        name="attention",
    )(*args)


def _outproj_kernel(att_ref, c_ref, x_ref, wa_ref, wc_ref, g2_ref, x1_ref, xn_ref):
    x1 = (x_ref[...]
          + jnp.dot(att_ref[...], wa_ref[...], preferred_element_type=F32)
          + jnp.dot(c_ref[...], wc_ref[...], preferred_element_type=F32))
    x1_ref[...] = x1
    ms = jnp.mean(x1 * x1, axis=-1, keepdims=True)
    xn_ref[...] = ((x1 * lax.rsqrt(ms + EPS)) * g2_ref[...]).astype(BF16)


def _outproj(att, c, x, wa, wc, g2):
    N, D = x.shape
    T = min(N, ROW_TILE)
    assert N % T == 0
    row = lambda w: pl.BlockSpec((T, w), lambda i: (i, 0))
    return pl.pallas_call(
        _outproj_kernel,
        grid=(N // T,),
        in_specs=[row(att.shape[1]), row(c.shape[1]), row(D),
                  _const_spec(wa.shape), _const_spec(wc.shape), _const_spec(g2.shape)],
        out_specs=[row(D), row(D)],
        out_shape=[jax.ShapeDtypeStruct((N, D), F32), jax.ShapeDtypeStruct((N, D), BF16)],
        compiler_params=_params(("parallel",)),
        name="outproj",
    )(att, c, x, wa, wc, g2)


def _candidate_pairs():
    return [(p, q) for p in range(1, PEER_TOPK + 1) for q in range(1, PEER_TOPK + 1)
            if p * q <= PEER_TOPK]


def _top_values(load, count, dst_ref):
    def one(p, prev):
        accs = [None] * 4
        for n in range(count):
            v = load(n)
            v = jnp.where(v < prev, v, -jnp.inf)
            a = accs[n % 4]
            accs[n % 4] = v if a is None else jnp.maximum(a, v)
        m = jnp.maximum(jnp.maximum(accs[0], accs[1]), jnp.maximum(accs[2], accs[3]))
        dst_ref[p] = m
        return m
    shape = dst_ref.shape[1:]
    return lax.fori_loop(0, PEER_TOPK, one, jnp.full(shape, jnp.inf, F32))


def _peer_route(xn_ref, wq_ref, kb_ref, s_ref, top_ref, cand_ref, kth_ref,
                r0_ref, e0_ref, cnt_ref, e1_ref, *, nk, ph, tokens):
    xn = xn_ref[...]
    for c in range(2):
        qt = lax.dot_general(wq_ref[c], xn, _NT, preferred_element_type=F32)
        sc = jnp.dot(kb_ref[c], qt.astype(BF16), preferred_element_type=F32)
        for lt in range(tokens // LANES):
            s_ref[c, lt] = sc[:, lt * LANES:(lt + 1) * LANES]

    for c in range(2):
        _top_values(lambda n, c=c: s_ref[c, :, n * ph:(n + 1) * ph, :], nk, top_ref.at[c])

    pairs = _candidate_pairs()
    for n, (p, q) in enumerate(pairs):
        cand_ref[n] = top_ref[0, p - 1] + top_ref[1, q - 1]
    theta = _top_values(lambda n: cand_ref[n], len(pairs), kth_ref)
    a1 = top_ref[0, 0]
    b1 = top_ref[1, 0]
    best = a1 + b1
    denom = None
    for n in range(len(pairs)):
        cv = cand_ref[n]
        t = jnp.where(cv >= theta, jnp.exp(cv - best), 0.0)
        denom = t if denom is None else denom + t
    inv = 1.0 / denom

    def first_half(i, _):
        rows = pl.ds(pl.multiple_of(i * ph, ph), ph)
        s0 = s_ref[0, :, rows, :]
        rank = jnp.ones_like(s0)
        for p in range(PEER_TOPK):
            rank = rank + jnp.where(top_ref[0, p] > s0, 1.0, 0.0)
        r0_ref[:, rows, :] = rank
        e0_ref[:, rows, :] = jnp.exp(s0 - a1) * inv
        return 0
    lax.fori_loop(0, nk, first_half, 0)

    def second_half(j, _):
        rows = pl.ds(pl.multiple_of(j * ph, ph), ph)
        s1 = s_ref[1, :, rows, :]
        cnt = jnp.zeros_like(s1)
        for p in range(PEER_TOPK):
            cnt = cnt + jnp.where(top_ref[0, p] + s1 >= theta, 1.0, 0.0)
        s_ref[1, :, rows, :] = cnt
        s_ref[0, :, rows, :] = jnp.exp(s1 - b1)
        return 0
    lax.fori_loop(0, nk, second_half, 0)
    for lt in range(tokens // LANES):
        for h in range(ph):
            cnt_ref[lt, h * nk:(h + 1) * nk, :] = (
                s_ref[1, lt, pl.ds(h, nk, stride=ph), :].astype(BF16))
            e1_ref[lt, h * nk:(h + 1) * nk, :] = (
                s_ref[0, lt, pl.ds(h, nk, stride=ph), :].astype(BF16))


def _zero_after(x):
    bits = pltpu.bitcast(x, jnp.uint32)
    bits = lax.shift_right_logical(lax.shift_right_logical(bits, jnp.uint32(16)), jnp.uint32(16))
    return pltpu.bitcast(bits, F32)


def _peer_gate_chunk(c, xu, coef_ref, r0_ref, e0_ref, cnt_ref, e1_ref, *, nk, ph, chunk, tokens):
    tile = (PEER_SUB, LANES)

    for ib in range(chunk // nk):
        i = c * (chunk // nk) + ib
        hrows = pl.ds(pl.multiple_of(i * ph, ph), ph)
        row0 = ib * nk
        for lt in range(tokens // LANES):
            lanes = slice(lt * LANES, (lt + 1) * LANES)
            zero = _zero_after(xu[row0:row0 + SUBLANES, lanes])
            rank = r0_ref[lt, hrows, :] + zero
            e0 = e0_ref[lt, hrows, :] + zero
            w = [None] * (nk // PEER_SUB)
            for h in range(ph):
                rank_b = jnp.broadcast_to(rank[h:h + 1], tile).astype(BF16)
                e0_b = jnp.broadcast_to(e0[h:h + 1], tile).astype(BF16)
                for jb in range(nk // PEER_SUB):
                    jrows = slice(h * nk + jb * PEER_SUB, h * nk + (jb + 1) * PEER_SUB)
                    t = jnp.where(rank_b <= cnt_ref[lt, jrows, :],
                                  e1_ref[lt, jrows, :], 0.0) * e0_b
                    w[jb] = t if w[jb] is None else w[jb] + t
            for jb in range(nk // PEER_SUB):
                erows = slice(row0 + jb * PEER_SUB, row0 + (jb + 1) * PEER_SUB)
                xs = xu[erows, lanes]
                act = 0.5 * xs * (1.0 + lax.erf(xs * math.sqrt(0.5)))
                coef_ref[erows, lanes] = w[jb] * act.astype(BF16)


def _peer_kernel(xn_ref, x1_ref, wq_ref, kb_ref, u_ref, vt_ref, y_ref,
                 s_ref, top_ref, cand_ref, kth_ref, r0_ref, e0_ref, cnt_ref, e1_ref,
                 acc_ref, coef_ref, *, nk, ph, chunk, tokens):
    e = pl.program_id(1)

    @pl.when(e == 0)
    def _():
        _peer_route(xn_ref, wq_ref, kb_ref, s_ref, top_ref, cand_ref, kth_ref,
                    r0_ref, e0_ref, cnt_ref, e1_ref, nk=nk, ph=ph, tokens=tokens)
        acc_ref[...] = jnp.zeros_like(acc_ref)

    xu = lax.dot_general(u_ref[...], xn_ref[...], _NT, preferred_element_type=F32)
    _peer_gate_chunk(e, xu, coef_ref, r0_ref, e0_ref, cnt_ref, e1_ref,
                     nk=nk, ph=ph, chunk=chunk, tokens=tokens)
    acc_ref[...] += jnp.dot(vt_ref[...], coef_ref[...], preferred_element_type=F32)

    @pl.when(e == pl.num_programs(1) - 1)
    def _():
        y_ref[...] = x1_ref[...] + acc_ref[...].T


def _peer(xn, x1, wq_t, kbig, u_bf, vt_bf):
    N, D = xn.shape
    E = u_bf.shape[0]
    ph = PEER_HEADS
    nk = kbig.shape[1] // ph
    T = min(N, PEER_TOKENS)
    chunk = min(E, PEER_EXPERT_CHUNK)
    assert N % T == 0 and E % chunk == 0 and chunk % nk == 0 and T % LANES == 0
    npairs = len(_candidate_pairs())
    lt = T // LANES
    once = dict(pipeline_mode=pl.Buffered(1))
    return pl.pallas_call(
        functools.partial(_peer_kernel, nk=nk, ph=ph, chunk=chunk, tokens=T),
        grid=(N // T, E // chunk),
        in_specs=[pl.BlockSpec((T, D), lambda t, e: (t, 0)),
                  pl.BlockSpec((T, D), lambda t, e: (t, 0), **once),
                  pl.BlockSpec(wq_t.shape, lambda t, e: (0, 0, 0), **once),
                  pl.BlockSpec(kbig.shape, lambda t, e: (0, 0, 0), **once),
                  pl.BlockSpec((chunk, D), lambda t, e: (e, 0)),
                  pl.BlockSpec((D, chunk), lambda t, e: (0, e))],
        out_specs=pl.BlockSpec((T, D), lambda t, e: (t, 0)),
        out_shape=jax.ShapeDtypeStruct((N, D), F32),
        scratch_shapes=[pltpu.VMEM((2, lt, nk * ph, LANES), F32),
                        pltpu.VMEM((2, PEER_TOPK, lt, ph, LANES), F32),
                        pltpu.VMEM((npairs, lt, ph, LANES), F32),
                        pltpu.VMEM((PEER_TOPK, lt, ph, LANES), F32),
                        pltpu.VMEM((lt, nk * ph, LANES), F32),
                        pltpu.VMEM((lt, nk * ph, LANES), F32),
                        pltpu.VMEM((lt, ph * nk, LANES), BF16),
                        pltpu.VMEM((lt, ph * nk, LANES), BF16),
                        pltpu.VMEM((D, T), F32),
                        pltpu.VMEM((chunk, T), BF16)],
        compiler_params=_params(("parallel", "arbitrary")),
        name="peer",
    )(xn, x1, wq_t, kbig, u_bf, vt_bf)


def _layer_weights(norm1_g, w_in, q_norm_g, k_norm_g, conv_w, conv_b, conv_ln_g, conv_ln_b,
                   w_out, norm2_g, peer_wq, peer_keys, peer_u, peer_v):
    att_w = (w_in.shape[1] - 2 * conv_w.shape[1]) // 3
    heads = att_w // HEAD_DIM
    row = lambda v: v.reshape(1, -1)
    head_of = jnp.arange(att_w) // HEAD_DIM
    gm = jnp.where(head_of[:, None] == head_of[None, :], 1.0 / HEAD_DIM, 0.0).astype(BF16)
    _, nk, half = peer_keys.shape
    ph = PEER_HEADS
    D = peer_wq.shape[0]
    wq_t = peer_wq.reshape(D, ph, 2, half).transpose(2, 1, 3, 0).reshape(2, ph * half, D)
    eye = jnp.eye(ph, dtype=peer_keys.dtype)
    kbig = (peer_keys[:, :, None, None, :] * eye[None, None, :, :, None]
            ).reshape(2, nk * ph, ph * half)
    return dict(
        g1=row(norm1_g), w_in=w_in.astype(BF16),
        gq=row(jnp.tile(q_norm_g, heads)), gk=row(jnp.tile(k_norm_g, heads)), gm=gm,
        cw=conv_w, cb=row(conv_b), lg=row(conv_ln_g), lb=row(conv_ln_b),
        wa=w_out[:att_w].astype(BF16), wc=w_out[att_w:].astype(BF16), g2=row(norm2_g),
        wq_t=wq_t.astype(BF16), kbig=kbig.astype(BF16),
        u=peer_u.astype(BF16), vt=peer_v.T.astype(BF16))


def _layer(x, past_k, past_v, conv_hist, w):
    B, S, D = x.shape
    q, k, v, glu = _inproj(x, w["g1"], w["w_in"], w["gq"], w["gk"], w["gm"])
    if past_k is None:
        att = _attention(q, k, v)
        conv_hist = jnp.zeros((B, w["cw"].shape[0] - 1, glu.shape[-1]), x.dtype)
    else:
        att = _attention(q, k, v, past_k, past_v)
    c, new_conv = _conv(glu, conv_hist, w["cw"], w["cb"], w["lg"], w["lb"])
    x1, xn = _outproj(att.reshape(B * S, -1), c.reshape(B * S, -1), x.reshape(B * S, D),
                      w["wa"], w["wc"], w["g2"])
    y = _peer(xn, x1, w["wq_t"], w["kbig"], w["u"], w["vt"])
    return y.reshape(B, S, D), k, v, new_conv


def kernel(x_prompt, x_sample, cache_k, cache_v, state_conv, norm1_g, w_in, q_norm_g, k_norm_g,
           conv_w, conv_b, conv_ln_g, conv_ln_b, w_out, norm2_g, peer_wq, peer_keys, peer_u,
           peer_v):
    y_p, y_s = x_prompt, x_sample
    outs = [[] for _ in range(6)]
    for l in range(norm1_g.shape[0]):
        w = _layer_weights(norm1_g[l], w_in[l], q_norm_g[l], k_norm_g[l], conv_w[l], conv_b[l],
                           conv_ln_g[l], conv_ln_b[l], w_out[l], norm2_g[l], peer_wq[l],
                           peer_keys[l], peer_u[l], peer_v[l])
        y_p, kp, vp, cp = _layer(y_p, None, None, None, w)
        y_s, ks, vs, cs = _layer(y_s, cache_k[l], cache_v[l], state_conv[l], w)
        for lst, val in zip(outs, (kp, vp, cp, ks, vs, cs)):
            lst.append(val)
    return (y_p, y_s) + tuple(jnp.stack(o) for o in outs)
```

```python
import functools
import math

import jax
import jax.numpy as jnp
from jax import lax
from jax.experimental import pallas as pl
from jax.experimental.pallas import tpu as pltpu

F32 = jnp.float32
BF16 = jnp.bfloat16

EPS = 1e-6
HEAD_DIM = 64
PEER_HEADS = 8
PEER_TOPK = 16
LANES = 128
SUBLANES = 8
ROW_TILE = 512
CONV_ROWS = 64
ATT_BLOCK = 128
PEER_TOKENS = 512
PEER_EXPERT_CHUNK = 1024
PEER_SUB = 16
PEER_MXU_ROWS = 256
VMEM_LIMIT = 60 * 1024 * 1024
DEAD_LOG = -120.0

_NT = (((1,), (1,)), ((), ()))


def _const_spec(shape):
    zeros = (0,) * len(shape)
    return pl.BlockSpec(shape, lambda *_: zeros)


def _params(semantics):
    return pltpu.CompilerParams(dimension_semantics=semantics, vmem_limit_bytes=VMEM_LIMIT)


def _inproj_kernel(x_ref, g1_ref, w_ref, gq_ref, gk_ref, gm_ref,
                   q_ref, k_ref, v_ref, glu_ref, *, heads, att_w, conv_ch):
    x = x_ref[0]
    ms = jnp.mean(x * x, axis=-1, keepdims=True)
    hb = ((x * lax.rsqrt(ms + EPS)) * g1_ref[...]).astype(BF16)

    def proj(c0, width):
        return jnp.dot(hb, w_ref[:, c0:c0 + width], preferred_element_type=F32)

    def head_norm(t, g):
        tt = t * t
        hi = tt.astype(BF16)
        lo = (tt - hi.astype(F32)).astype(BF16)
        msq = (jnp.dot(hi, gm_ref[...], preferred_element_type=F32)
               + jnp.dot(lo, gm_ref[...], preferred_element_type=F32))
        return (t * lax.rsqrt(msq + EPS)) * g

    qn = head_norm(proj(0, att_w), gq_ref[...]) * (HEAD_DIM ** -0.5)
    kn = head_norm(proj(att_w, att_w), gk_ref[...])
    v = proj(2 * att_w, att_w)
    for h in range(heads):
        sl = slice(h * HEAD_DIM, (h + 1) * HEAD_DIM)
        q_ref[0, h] = qn[:, sl].astype(BF16)
        k_ref[0, h] = kn[:, sl]
        v_ref[0, h] = v[:, sl]
    a = proj(3 * att_w, conv_ch)
    gate = proj(3 * att_w + conv_ch, conv_ch)
    glu_ref[0] = a * jax.nn.sigmoid(gate)


def _inproj(x, g1, w_bf, gq, gk, gm):
    B, S, D = x.shape
    att_w = gq.shape[-1]
    heads = att_w // HEAD_DIM
    conv_ch = (w_bf.shape[1] - 3 * att_w) // 2
    T = min(S, ROW_TILE)
    assert S % T == 0
    hm = lambda dt: jax.ShapeDtypeStruct((B, heads, S, HEAD_DIM), dt)
    hspec = pl.BlockSpec((1, heads, T, HEAD_DIM), lambda b, s: (b, 0, s, 0))
    return pl.pallas_call(
        functools.partial(_inproj_kernel, heads=heads, att_w=att_w, conv_ch=conv_ch),
        grid=(B, S // T),
        in_specs=[pl.BlockSpec((1, T, D), lambda b, s: (b, s, 0)),
                  _const_spec(g1.shape), _const_spec(w_bf.shape),
                  _const_spec(gq.shape), _const_spec(gk.shape), _const_spec(gm.shape)],
        out_specs=[hspec, hspec, hspec, pl.BlockSpec((1, T, conv_ch), lambda b, s: (b, s, 0))],
        out_shape=[hm(BF16), hm(F32), hm(F32), jax.ShapeDtypeStruct((B, S, conv_ch), F32)],
        compiler_params=_params(("parallel", "parallel")),
        name="inproj",
    )(x, g1, w_bf, gq, gk, gm)


def _conv_kernel(glu_ref, hist_ref, w_ref, b_ref, lg_ref, lb_ref, c_ref, new_ref, ext_ref,
                 *, rows, width, chunk):
    hist = width - 1
    base = 32
    off = base - hist
    s = pl.program_id(1)

    @pl.when(s == 0)
    def _():
        ext_ref[off:base, :] = hist_ref[0]
        ext_ref[base + rows:, :] = jnp.zeros((SUBLANES, ext_ref.shape[1]), F32)

    ext_ref[base:base + rows, :] = glu_ref[0]
    span = chunk + SUBLANES
    lanes_all = ext_ref.shape[1]
    for r0 in range(0, rows, chunk):
        cols = []
        for l0 in range(0, lanes_all, LANES):
            acc = None
            for b in range(SUBLANES):
                part = None
                for a in range((off + width - 1) // SUBLANES + 1):
                    w = SUBLANES * a + b - off
                    if 0 <= w < width:
                        term = (ext_ref[r0 + SUBLANES * a:r0 + SUBLANES * a + span, l0:l0 + LANES]
                                * w_ref[w:w + 1, l0:l0 + LANES])
                        part = term if part is None else part + term
                if part is not None:
                    acc = part[b:b + chunk] if acc is None else acc + part[b:b + chunk]
            cols.append(acc)
        acc = jnp.concatenate(cols, axis=1) + b_ref[...]
        mu = jnp.mean(acc, axis=-1, keepdims=True)
        d = acc - mu
        var = jnp.mean(d * d, axis=-1, keepdims=True)
        y = d * lax.rsqrt(var + EPS) * lg_ref[...] + lb_ref[...]
        c_ref[0, r0:r0 + chunk, :] = (y * jax.nn.sigmoid(y)).astype(BF16)
    tail = ext_ref[rows + off:rows + base, :]
    new_ref[0] = tail
    ext_ref[off:base, :] = tail


def _conv(glu, hist, cw, cb, lg, lb):
    B, S, C = glu.shape
    width = cw.shape[0]
    T = min(S, ROW_TILE)
    assert S % T == 0 and (T >= width - 1 or S == T)
    chunk = min(T, CONV_ROWS)
    return pl.pallas_call(
        functools.partial(_conv_kernel, rows=T, width=width, chunk=chunk),
        grid=(B, S // T),
        in_specs=[pl.BlockSpec((1, T, C), lambda b, s: (b, s, 0)),
                  pl.BlockSpec((1, width - 1, C), lambda b, s: (b, 0, 0)),
                  _const_spec(cw.shape), _const_spec(cb.shape),
                  _const_spec(lg.shape), _const_spec(lb.shape)],
        out_specs=[pl.BlockSpec((1, T, C), lambda b, s: (b, s, 0)),
                   pl.BlockSpec((1, width - 1, C), lambda b, s: (b, 0, 0))],
        out_shape=[jax.ShapeDtypeStruct((B, S, C), BF16),
                   jax.ShapeDtypeStruct((B, width - 1, C), F32)],
        scratch_shapes=[pltpu.VMEM((32 + T + SUBLANES, C), F32)],
        compiler_params=_params(("parallel", "arbitrary")),
        name="conv",
    )(glu, hist, cw, cb, lg, lb)


def _sb_block(qs, kbs, vbs, carries, upper, ones, mask):
    tq, tk = qs[0].shape[0], kbs[0].shape[0]
    zs = [lax.dot_general(q, kb, _NT, preferred_element_type=F32) for q, kb in zip(qs, kbs)]
    log_betas, sums = [], []
    for z in zs:
        sp = jnp.log1p(jnp.exp(-jnp.abs(z)))
        log_beta = jnp.minimum(z, 0.0) - sp
        log_1m = log_beta - z
        if mask is not None:
            log_1m = jnp.where(mask, log_1m, 0.0)
        hi = log_1m.astype(BF16)
        lo = (log_1m - hi.astype(F32)).astype(BF16)
        both = jnp.concatenate([hi, lo], axis=0)
        if tk == LANES:
            p = jnp.dot(both, jnp.concatenate([upper, ones], axis=1),
                        preferred_element_type=F32)
            after, total = p[:, :tk], p[:, tk:]
        else:
            after = jnp.dot(both, upper, preferred_element_type=F32)
            total = jnp.dot(both, ones, preferred_element_type=F32)
        log_betas.append(log_beta)
        sums.append((after[:tq] + after[tq:], total[:tq] + total[tq:]))
    outs, new_carries = [], []
    for log_beta, (after, total), carry, vb in zip(log_betas, sums, carries, vbs):
        logw = log_beta + after
        if carry is not None:
            logw = logw + (carry if tk == LANES else carry[:, :1])
        w = jnp.exp(logw)
        if mask is not None:
            w = jnp.where(mask, w, 0.0)
        outs.append(jnp.dot(w.astype(BF16), vb, preferred_element_type=F32))
        new_carries.append(total if carry is None else carry + total)
    return outs, new_carries


def _max_all(xs):
    m = xs[0]
    for x in xs[1:]:
        m = jnp.maximum(m, x)
    return jnp.max(m)


def _attn_kernel(*refs, heads, tq, tk, old_blocks, diag_in_old):
    if diag_in_old:
        q_ref, ko_ref, vo_ref, o_ref, acc_ref, carry_ref = refs
    else:
        q_ref, kd_ref, vd_ref, ko_ref, vo_ref, o_ref, acc_ref, carry_ref = refs
    qi = pl.program_id(1)

    def tri(n):
        r = lax.broadcasted_iota(jnp.int32, (n, n), 0)
        c = lax.broadcasted_iota(jnp.int32, (n, n), 1)
        return r, c

    r, c = tri(tq)
    mask = c < r
    upper_d = jnp.where(r > c, 1.0, 0.0).astype(BF16)
    ones_d = jnp.ones((tq, LANES), BF16)
    hs = range(heads)
    qs = [q_ref[0, h] for h in hs]
    if diag_in_old:
        rows = pl.ds(pl.multiple_of(qi * tq, tq), tq)
        kbs = [ko_ref[0, h, rows, :].astype(BF16) for h in hs]
        vbs = [vo_ref[0, h, rows, :].astype(BF16) for h in hs]
    else:
        kbs = [kd_ref[0, h].astype(BF16) for h in hs]
        vbs = [vd_ref[0, h].astype(BF16) for h in hs]
    outs, carries = _sb_block(qs, kbs, vbs, [None] * heads, upper_d, ones_d, mask)
    for h in hs:
        acc_ref[h] = outs[h]
        carry_ref[h] = carries[h]

    if old_blocks is None:
        j0 = qi - 1
    else:
        j0 = jnp.int32(old_blocks - 1)

    r, c = tri(tk)
    upper_o = jnp.where(r > c, 1.0, 0.0).astype(BF16)
    ones_o = jnp.ones((tk, LANES), BF16)

    def cond(state):
        j, live = state
        return jnp.logical_and(j >= 0, live > DEAD_LOG)

    def body(state):
        j, _ = state
        rows = pl.ds(pl.multiple_of(j * tk, tk), tk)
        qs = [q_ref[0, h] for h in hs]
        kbs = [ko_ref[0, h, rows, :].astype(BF16) for h in hs]
        vbs = [vo_ref[0, h, rows, :].astype(BF16) for h in hs]
        old = [carry_ref[h] for h in hs]
        accs = [acc_ref[h] for h in hs]
        outs, new = _sb_block(qs, kbs, vbs, old, upper_o, ones_o, None)
        for h in hs:
            acc_ref[h] = accs[h] + outs[h]
            carry_ref[h] = new[h]
        return j - 1, _max_all(new)

    lax.while_loop(cond, body, (j0, _max_all(carries)))
    for h in range(heads):
        o_ref[0, :, h * HEAD_DIM:(h + 1) * HEAD_DIM] = acc_ref[h].astype(BF16)


def _attention(q, k_new, v_new, k_old=None, v_old=None):
    B, H, S, d = q.shape
    if k_old is None:
        tq = tk = min(S, ATT_BLOCK)
        assert S % tq == 0
        old_blocks, diag_in_old = None, True
        full = pl.BlockSpec((1, H, S, d), lambda b, i: (b, 0, 0, 0))
        in_specs = [pl.BlockSpec((1, H, tq, d), lambda b, i: (b, 0, i, 0)), full, full]
        args = (q, k_new, v_new)
    else:
        P = k_old.shape[2]
        tq, tk = S, min(P, ATT_BLOCK)
        assert P % tk == 0
        old_blocks, diag_in_old = P // tk, False
        new = pl.BlockSpec((1, H, S, d), lambda b, i: (b, 0, 0, 0))
        old = pl.BlockSpec((1, H, P, d), lambda b, i: (b, 0, 0, 0))
        in_specs = [new, new, new, old, old]
        args = (q, k_new, v_new, k_old, v_old)
    return pl.pallas_call(
        functools.partial(_attn_kernel, heads=H, tq=tq, tk=tk, old_blocks=old_blocks,
                          diag_in_old=diag_in_old),
        grid=(B, S // tq),
        in_specs=in_specs,
        out_specs=pl.BlockSpec((1, tq, H * d), lambda b, i: (b, i, 0)),
        out_shape=jax.ShapeDtypeStruct((B, S, H * d), BF16),
        scratch_shapes=[pltpu.VMEM((H, tq, d), F32), pltpu.VMEM((H, tq, LANES), F32)],
        compiler_params=_params(("parallel", "arbitrary")),
        name="attention",
    )(*args)


def _outproj_kernel(att_ref, c_ref, x_ref, wa_ref, wc_ref, g2_ref, x1_ref, xn_ref):
    x1 = (x_ref[...]
          + jnp.dot(att_ref[...], wa_ref[...], preferred_element_type=F32)
          + jnp.dot(c_ref[...], wc_ref[...], preferred_element_type=F32))
    x1_ref[...] = x1
    ms = jnp.mean(x1 * x1, axis=-1, keepdims=True)
    xn_ref[...] = ((x1 * lax.rsqrt(ms + EPS)) * g2_ref[...]).astype(BF16)


def _outproj(att, c, x, wa, wc, g2):
    N, D = x.shape
    T = min(N, ROW_TILE)
    assert N % T == 0
    row = lambda w: pl.BlockSpec((T, w), lambda i: (i, 0))
    return pl.pallas_call(
        _outproj_kernel,
        grid=(N // T,),
        in_specs=[row(att.shape[1]), row(c.shape[1]), row(D),
                  _const_spec(wa.shape), _const_spec(wc.shape), _const_spec(g2.shape)],
        out_specs=[row(D), row(D)],
        out_shape=[jax.ShapeDtypeStruct((N, D), F32), jax.ShapeDtypeStruct((N, D), BF16)],
        compiler_params=_params(("parallel",)),
        name="outproj",
    )(att, c, x, wa, wc, g2)


def _candidate_pairs():
    return [(p, q) for p in range(1, PEER_TOPK + 1) for q in range(1, PEER_TOPK + 1)
            if p * q <= PEER_TOPK]


def _sort_desc(v):
    v = list(v)
    n = len(v)
    k = 2
    while k <= n:
        j = k // 2
        while j >= 1:
            for i in range(n):
                m = i ^ j
                if m > i:
                    hi, lo = jnp.maximum(v[i], v[m]), jnp.minimum(v[i], v[m])
                    v[i], v[m] = (hi, lo) if (i & k) == 0 else (lo, hi)
            j //= 2
        k *= 2
    return v


def _merge_top(a, b):
    n = len(a)
    c = [jnp.maximum(a[i], b[n - 1 - i]) for i in range(n)]
    j = n // 2
    while j >= 1:
        for i in range(n):
            m = i ^ j
            if m > i:
                c[i], c[m] = jnp.maximum(c[i], c[m]), jnp.minimum(c[i], c[m])
        j //= 2
    return c


def _top_values(load, count):
    k = PEER_TOPK
    groups = []
    for g in range(0, count, k):
        vals = [load(n) for n in range(g, min(g + k, count))]
        vals += [jnp.full_like(vals[0], -jnp.inf)] * (k - len(vals))
        groups.append(_sort_desc(vals))
    while len(groups) > 1:
        merged = [_merge_top(groups[i], groups[i + 1]) for i in range(0, len(groups) - 1, 2)]
        groups = merged + ([groups[-1]] if len(groups) % 2 else [])
    return groups[0]


def _peer_route(xn_ref, wq_ref, kb_ref, s_ref, top_ref, cand_ref, kth_ref,
                r0_ref, e0_ref, cnt_ref, e1_ref, *, nk, ph, tokens):
    xn = xn_ref[...]
    for c in range(2):
        qt = lax.dot_general(wq_ref[c], xn, _NT, preferred_element_type=F32)
        sc = jnp.dot(kb_ref[c], qt.astype(BF16), preferred_element_type=F32)
        for lt in range(tokens // LANES):
            s_ref[c, lt] = sc[:, lt * LANES:(lt + 1) * LANES]

    pairs = _candidate_pairs()
    for lt in range(tokens // LANES):
        for c in range(2):
            top = _top_values(lambda n: s_ref[c, lt, n * ph:(n + 1) * ph, :], nk)
            for p in range(PEER_TOPK):
                top_ref[c, p, lt] = top[p]
        for n, (p, q) in enumerate(pairs):
            cand_ref[n, lt] = top_ref[0, p - 1, lt] + top_ref[1, q - 1, lt]
        kth_ref[lt] = _top_values(lambda n: cand_ref[n, lt], len(pairs))[PEER_TOPK - 1]
    theta = kth_ref[...]
    a1 = top_ref[0, 0]
    b1 = top_ref[1, 0]
    best = a1 + b1
    denom = None
    for n in range(len(pairs)):
        cv = cand_ref[n]
        t = jnp.where(cv >= theta, jnp.exp(cv - best), 0.0)
        denom = t if denom is None else denom + t
    inv = 1.0 / denom

    def first_half(i, _):
        rows = pl.ds(pl.multiple_of(i * ph, ph), ph)
        s0 = s_ref[0, :, rows, :]
        rank = jnp.ones_like(s0)
        for p in range(PEER_TOPK):
            rank = rank + jnp.where(top_ref[0, p] > s0, 1.0, 0.0)
        r0_ref[:, rows, :] = rank
        e0_ref[:, rows, :] = jnp.exp(s0 - a1) * inv
        return 0
    lax.fori_loop(0, nk, first_half, 0)

    def second_half(j, _):
        rows = pl.ds(pl.multiple_of(j * ph, ph), ph)
        s1 = s_ref[1, :, rows, :]
        cnt = jnp.zeros_like(s1)
        for p in range(PEER_TOPK):
            cnt = cnt + jnp.where(top_ref[0, p] + s1 >= theta, 1.0, 0.0)
        s_ref[1, :, rows, :] = cnt
        s_ref[0, :, rows, :] = jnp.exp(s1 - b1)
        return 0
    lax.fori_loop(0, nk, second_half, 0)
    for lt in range(tokens // LANES):
        for h in range(ph):
            cnt_ref[lt, h * nk:(h + 1) * nk, :] = (
                s_ref[1, lt, pl.ds(h, nk, stride=ph), :].astype(BF16))
            e1_ref[lt, h * nk:(h + 1) * nk, :] = (
                s_ref[0, lt, pl.ds(h, nk, stride=ph), :].astype(BF16))


def _zero_after(x):
    bits = pltpu.bitcast(x, jnp.uint32)
    bits = lax.shift_right_logical(lax.shift_right_logical(bits, jnp.uint32(16)), jnp.uint32(16))
    return pltpu.bitcast(bits, F32)


def _peer_gate_chunk(c, xu, coef_ref, r0_ref, e0_ref, cnt_ref, e1_ref, *, nk, ph, chunk, tokens):
    tile = (PEER_SUB, LANES)

    for ib in range(chunk // nk):
        i = c * (chunk // nk) + ib
        hrows = pl.ds(pl.multiple_of(i * ph, ph), ph)
        row0 = ib * nk
        for lt in range(tokens // LANES):
            lanes = slice(lt * LANES, (lt + 1) * LANES)
            zero = _zero_after(xu[row0:row0 + SUBLANES, lanes])
            rank = r0_ref[lt, hrows, :] + zero
            e0 = e0_ref[lt, hrows, :] + zero
            w = [None] * (nk // PEER_SUB)
            for h in range(ph):
                rank_b = jnp.broadcast_to(rank[h:h + 1], tile).astype(BF16)
                e0_b = jnp.broadcast_to(e0[h:h + 1], tile).astype(BF16)
                for jb in range(nk // PEER_SUB):
                    jrows = slice(h * nk + jb * PEER_SUB, h * nk + (jb + 1) * PEER_SUB)
                    t = jnp.where(rank_b <= cnt_ref[lt, jrows, :],
                                  e1_ref[lt, jrows, :], 0.0) * e0_b
                    w[jb] = t if w[jb] is None else w[jb] + t
            for jb in range(nk // PEER_SUB):
                erows = slice(row0 + jb * PEER_SUB, row0 + (jb + 1) * PEER_SUB)
                xs = xu[erows, lanes]
                act = 0.5 * xs * (1.0 + lax.erf(xs * math.sqrt(0.5)))
                coef_ref[erows, lanes] = w[jb] * act.astype(BF16)


def _peer_kernel(xn_ref, x1_ref, wq_ref, kb_ref, u_ref, vt_ref, y_ref,
                 s_ref, top_ref, cand_ref, kth_ref, r0_ref, e0_ref, cnt_ref, e1_ref,
                 acc_ref, coef_ref, *, nk, ph, chunk, tokens):
    e = pl.program_id(1)

    @pl.when(e == 0)
    def _():
        _peer_route(xn_ref, wq_ref, kb_ref, s_ref, top_ref, cand_ref, kth_ref,
                    r0_ref, e0_ref, cnt_ref, e1_ref, nk=nk, ph=ph, tokens=tokens)
        acc_ref[...] = jnp.zeros_like(acc_ref)

    xn = xn_ref[...]
    groups = chunk // PEER_MXU_ROWS
    for g in range(groups):
        rows = slice(g * PEER_MXU_ROWS, (g + 1) * PEER_MXU_ROWS)
        xu = lax.dot_general(u_ref[rows, :], xn, _NT, preferred_element_type=F32)
        _peer_gate_chunk(e * groups + g, xu, coef_ref.at[rows, :], r0_ref, e0_ref, cnt_ref, e1_ref,
                         nk=nk, ph=ph, chunk=PEER_MXU_ROWS, tokens=tokens)
    acc_ref[...] += jnp.dot(vt_ref[...], coef_ref[...], preferred_element_type=F32)

    @pl.when(e == pl.num_programs(1) - 1)
    def _():
        y_ref[...] = x1_ref[...] + acc_ref[...].T


def _peer(xn, x1, wq_t, kbig, u_bf, vt_bf):
    N, D = xn.shape
    E = u_bf.shape[0]
    ph = PEER_HEADS
    nk = kbig.shape[1] // ph
    T = min(N, PEER_TOKENS)
    chunk = min(E, PEER_EXPERT_CHUNK)
    assert N % T == 0 and E % chunk == 0 and T % LANES == 0
    assert chunk % PEER_MXU_ROWS == 0 and PEER_MXU_ROWS % nk == 0
    npairs = len(_candidate_pairs())
    lt = T // LANES
    once = dict(pipeline_mode=pl.Buffered(1))
    return pl.pallas_call(
        functools.partial(_peer_kernel, nk=nk, ph=ph, chunk=chunk, tokens=T),
        grid=(N // T, E // chunk),
        in_specs=[pl.BlockSpec((T, D), lambda t, e: (t, 0)),
                  pl.BlockSpec((T, D), lambda t, e: (t, 0), **once),
                  pl.BlockSpec(wq_t.shape, lambda t, e: (0, 0, 0), **once),
                  pl.BlockSpec(kbig.shape, lambda t, e: (0, 0, 0), **once),
                  pl.BlockSpec((chunk, D), lambda t, e: (e, 0)),
                  pl.BlockSpec((D, chunk), lambda t, e: (0, e))],
        out_specs=pl.BlockSpec((T, D), lambda t, e: (t, 0)),
        out_shape=jax.ShapeDtypeStruct((N, D), F32),
        scratch_shapes=[pltpu.VMEM((2, lt, nk * ph, LANES), F32),
                        pltpu.VMEM((2, PEER_TOPK, lt, ph, LANES), F32),
                        pltpu.VMEM((npairs, lt, ph, LANES), F32),
                        pltpu.VMEM((lt, ph, LANES), F32),
                        pltpu.VMEM((lt, nk * ph, LANES), F32),
                        pltpu.VMEM((lt, nk * ph, LANES), F32),
                        pltpu.VMEM((lt, ph * nk, LANES), BF16),
                        pltpu.VMEM((lt, ph * nk, LANES), BF16),
                        pltpu.VMEM((D, T), F32),
                        pltpu.VMEM((chunk, T), BF16)],
        compiler_params=_params(("parallel", "arbitrary")),
        name="peer",
    )(xn, x1, wq_t, kbig, u_bf, vt_bf)


def _layer_weights(norm1_g, w_in, q_norm_g, k_norm_g, conv_w, conv_b, conv_ln_g, conv_ln_b,
                   w_out, norm2_g, peer_wq, peer_keys, peer_u, peer_v):
    att_w = (w_in.shape[1] - 2 * conv_w.shape[1]) // 3
    heads = att_w // HEAD_DIM
    row = lambda v: v.reshape(1, -1)
    head_of = jnp.arange(att_w) // HEAD_DIM
    gm = jnp.where(head_of[:, None] == head_of[None, :], 1.0 / HEAD_DIM, 0.0).astype(BF16)
    _, nk, half = peer_keys.shape
    ph = PEER_HEADS
    D = peer_wq.shape[0]
    wq_t = peer_wq.reshape(D, ph, 2, half).transpose(2, 1, 3, 0).reshape(2, ph * half, D)
    eye = jnp.eye(ph, dtype=peer_keys.dtype)
    kbig = (peer_keys[:, :, None, None, :] * eye[None, None, :, :, None]
            ).reshape(2, nk * ph, ph * half)
    return dict(
        g1=row(norm1_g), w_in=w_in.astype(BF16),
        gq=row(jnp.tile(q_norm_g, heads)), gk=row(jnp.tile(k_norm_g, heads)), gm=gm,
        cw=conv_w, cb=row(conv_b), lg=row(conv_ln_g), lb=row(conv_ln_b),
        wa=w_out[:att_w].astype(BF16), wc=w_out[att_w:].astype(BF16), g2=row(norm2_g),
        wq_t=wq_t.astype(BF16), kbig=kbig.astype(BF16),
        u=peer_u.astype(BF16), vt=peer_v.T.astype(BF16))


def _layer(x, past_k, past_v, conv_hist, w):
    B, S, D = x.shape
    q, k, v, glu = _inproj(x, w["g1"], w["w_in"], w["gq"], w["gk"], w["gm"])
    if past_k is None:
        att = _attention(q, k, v)
        conv_hist = jnp.zeros((B, w["cw"].shape[0] - 1, glu.shape[-1]), x.dtype)
    else:
        att = _attention(q, k, v, past_k, past_v)
    c, new_conv = _conv(glu, conv_hist, w["cw"], w["cb"], w["lg"], w["lb"])
    x1, xn = _outproj(att.reshape(B * S, -1), c.reshape(B * S, -1), x.reshape(B * S, D),
                      w["wa"], w["wc"], w["g2"])
    y = _peer(xn, x1, w["wq_t"], w["kbig"], w["u"], w["vt"])
    return y.reshape(B, S, D), k, v, new_conv


def kernel(x_prompt, x_sample, cache_k, cache_v, state_conv, norm1_g, w_in, q_norm_g, k_norm_g,
           conv_w, conv_b, conv_ln_g, conv_ln_b, w_out, norm2_g, peer_wq, peer_keys, peer_u,
           peer_v):
    y_p, y_s = x_prompt, x_sample
    outs = [[] for _ in range(6)]
    for l in range(norm1_g.shape[0]):
        w = _layer_weights(norm1_g[l], w_in[l], q_norm_g[l], k_norm_g[l], conv_w[l], conv_b[l],
                           conv_ln_g[l], conv_ln_b[l], w_out[l], norm2_g[l], peer_wq[l],
                           peer_keys[l], peer_u[l], peer_v[l])
        y_p, kp, vp, cp = _layer(y_p, None, None, None, w)
        y_s, ks, vs, cs = _layer(y_s, cache_k[l], cache_v[l], state_conv[l], w)
        for lst, val in zip(outs, (kp, vp, cp, ks, vs, cs)):
            lst.append(val)
    return (y_p, y_s) + tuple(jnp.stack(o) if len(o) > 1 else o[0][None] for o in outs)
```

```python
import functools
import math

import jax
import jax.numpy as jnp
from jax import lax
from jax.experimental import pallas as pl
from jax.experimental.pallas import tpu as pltpu

F32 = jnp.float32
BF16 = jnp.bfloat16

EPS = 1e-6
HEAD_DIM = 64
PEER_HEADS = 8
PEER_TOPK = 16
LANES = 128
SUBLANES = 8
ROW_TILE = 512
CONV_ROWS = 64
ATT_BLOCK = 128
PEER_TOKENS = 512
PEER_EXPERT_CHUNK = 1024
PEER_SUB = 16
VMEM_LIMIT = 60 * 1024 * 1024
DEAD_LOG = -120.0

_NT = (((1,), (1,)), ((), ()))


def _const_spec(shape):
    zeros = (0,) * len(shape)
    return pl.BlockSpec(shape, lambda *_: zeros)


def _params(semantics):
    return pltpu.CompilerParams(dimension_semantics=semantics, vmem_limit_bytes=VMEM_LIMIT)


def _inproj_kernel(x_ref, g1_ref, w_ref, gq_ref, gk_ref, gm_ref,
                   q_ref, k_ref, v_ref, glu_ref, *, heads, att_w, conv_ch):
    x = x_ref[0]
    ms = jnp.mean(x * x, axis=-1, keepdims=True)
    hb = ((x * lax.rsqrt(ms + EPS)) * g1_ref[...]).astype(BF16)

    def proj(c0, width):
        return jnp.dot(hb, w_ref[:, c0:c0 + width], preferred_element_type=F32)

    def head_norm(t, g):
        tt = t * t
        hi = tt.astype(BF16)
        lo = (tt - hi.astype(F32)).astype(BF16)
        msq = (jnp.dot(hi, gm_ref[...], preferred_element_type=F32)
               + jnp.dot(lo, gm_ref[...], preferred_element_type=F32))
        return (t * lax.rsqrt(msq + EPS)) * g

    qn = head_norm(proj(0, att_w), gq_ref[...]) * (HEAD_DIM ** -0.5)
    kn = head_norm(proj(att_w, att_w), gk_ref[...])
    v = proj(2 * att_w, att_w)
    for h in range(heads):
        sl = slice(h * HEAD_DIM, (h + 1) * HEAD_DIM)
        q_ref[0, h] = qn[:, sl].astype(BF16)
        k_ref[0, h] = kn[:, sl]
        v_ref[0, h] = v[:, sl]
    a = proj(3 * att_w, conv_ch)
    gate = proj(3 * att_w + conv_ch, conv_ch)
    glu_ref[0] = a * jax.nn.sigmoid(gate)


def _inproj(x, g1, w_bf, gq, gk, gm):
    B, S, D = x.shape
    att_w = gq.shape[-1]
    heads = att_w // HEAD_DIM
    conv_ch = (w_bf.shape[1] - 3 * att_w) // 2
    T = min(S, ROW_TILE)
    assert S % T == 0
    hm = lambda dt: jax.ShapeDtypeStruct((B, heads, S, HEAD_DIM), dt)
    hspec = pl.BlockSpec((1, heads, T, HEAD_DIM), lambda b, s: (b, 0, s, 0))
    return pl.pallas_call(
        functools.partial(_inproj_kernel, heads=heads, att_w=att_w, conv_ch=conv_ch),
        grid=(B, S // T),
        in_specs=[pl.BlockSpec((1, T, D), lambda b, s: (b, s, 0)),
                  _const_spec(g1.shape), _const_spec(w_bf.shape),
                  _const_spec(gq.shape), _const_spec(gk.shape), _const_spec(gm.shape)],
        out_specs=[hspec, hspec, hspec, pl.BlockSpec((1, T, conv_ch), lambda b, s: (b, s, 0))],
        out_shape=[hm(BF16), hm(F32), hm(F32), jax.ShapeDtypeStruct((B, S, conv_ch), F32)],
        compiler_params=_params(("parallel", "parallel")),
        name="inproj",
    )(x, g1, w_bf, gq, gk, gm)


def _conv_kernel(glu_ref, hist_ref, w_ref, b_ref, lg_ref, lb_ref, c_ref, new_ref, ext_ref,
                 *, rows, width, chunk):
    hist = width - 1
    base = 32
    off = base - hist
    s = pl.program_id(1)

    @pl.when(s == 0)
    def _():
        ext_ref[off:base, :] = hist_ref[0]
        ext_ref[base + rows:, :] = jnp.zeros((SUBLANES, ext_ref.shape[1]), F32)

    ext_ref[base:base + rows, :] = glu_ref[0]
    span = chunk + SUBLANES
    lanes_all = ext_ref.shape[1]
    for r0 in range(0, rows, chunk):
        cols = []
        for l0 in range(0, lanes_all, LANES):
            acc = None
            for b in range(SUBLANES):
                part = None
                for a in range((off + width - 1) // SUBLANES + 1):
                    w = SUBLANES * a + b - off
                    if 0 <= w < width:
                        term = (ext_ref[r0 + SUBLANES * a:r0 + SUBLANES * a + span, l0:l0 + LANES]
                                * w_ref[w:w + 1, l0:l0 + LANES])
                        part = term if part is None else part + term
                if part is not None:
                    acc = part[b:b + chunk] if acc is None else acc + part[b:b + chunk]
            cols.append(acc)
        acc = jnp.concatenate(cols, axis=1) + b_ref[...]
        mu = jnp.mean(acc, axis=-1, keepdims=True)
        d = acc - mu
        var = jnp.mean(d * d, axis=-1, keepdims=True)
        y = d * lax.rsqrt(var + EPS) * lg_ref[...] + lb_ref[...]
        c_ref[0, r0:r0 + chunk, :] = (y * jax.nn.sigmoid(y)).astype(BF16)
    tail = ext_ref[rows + off:rows + base, :]
    new_ref[0] = tail
    ext_ref[off:base, :] = tail


def _conv(glu, hist, cw, cb, lg, lb):
    B, S, C = glu.shape
    width = cw.shape[0]
    T = min(S, ROW_TILE)
    assert S % T == 0 and (T >= width - 1 or S == T)
    chunk = min(T, CONV_ROWS)
    return pl.pallas_call(
        functools.partial(_conv_kernel, rows=T, width=width, chunk=chunk),
        grid=(B, S // T),
        in_specs=[pl.BlockSpec((1, T, C), lambda b, s: (b, s, 0)),
                  pl.BlockSpec((1, width - 1, C), lambda b, s: (b, 0, 0)),
                  _const_spec(cw.shape), _const_spec(cb.shape),
                  _const_spec(lg.shape), _const_spec(lb.shape)],
        out_specs=[pl.BlockSpec((1, T, C), lambda b, s: (b, s, 0)),
                   pl.BlockSpec((1, width - 1, C), lambda b, s: (b, 0, 0))],
        out_shape=[jax.ShapeDtypeStruct((B, S, C), BF16),
                   jax.ShapeDtypeStruct((B, width - 1, C), F32)],
        scratch_shapes=[pltpu.VMEM((32 + T + SUBLANES, C), F32)],
        compiler_params=_params(("parallel", "arbitrary")),
        name="conv",
    )(glu, hist, cw, cb, lg, lb)


def _sb_block(qs, kbs, vbs, carries, upper, ones, mask):
    tq, tk = qs[0].shape[0], kbs[0].shape[0]
    zs = [lax.dot_general(q, kb, _NT, preferred_element_type=F32) for q, kb in zip(qs, kbs)]
    log_betas, sums = [], []
    for z in zs:
        sp = jnp.log1p(jnp.exp(-jnp.abs(z)))
        log_beta = jnp.minimum(z, 0.0) - sp
        log_1m = log_beta - z
        if mask is not None:
            log_1m = jnp.where(mask, log_1m, 0.0)
        hi = log_1m.astype(BF16)
        lo = (log_1m - hi.astype(F32)).astype(BF16)
        both = jnp.concatenate([hi, lo], axis=0)
        if tk == LANES:
            p = jnp.dot(both, jnp.concatenate([upper, ones], axis=1),
                        preferred_element_type=F32)
            after, total = p[:, :tk], p[:, tk:]
        else:
            after = jnp.dot(both, upper, preferred_element_type=F32)
            total = jnp.dot(both, ones, preferred_element_type=F32)
        log_betas.append(log_beta)
        sums.append((after[:tq] + after[tq:], total[:tq] + total[tq:]))
    outs, new_carries = [], []
    for log_beta, (after, total), carry, vb in zip(log_betas, sums, carries, vbs):
        logw = log_beta + after
        if carry is not None:
            logw = logw + (carry if tk == LANES else carry[:, :1])
        w = jnp.exp(logw)
        if mask is not None:
            w = jnp.where(mask, w, 0.0)
        outs.append(jnp.dot(w.astype(BF16), vb, preferred_element_type=F32))
        new_carries.append(total if carry is None else carry + total)
    return outs, new_carries


def _max_all(xs):
    m = xs[0]
    for x in xs[1:]:
        m = jnp.maximum(m, x)
    return jnp.max(m)


def _attn_kernel(*refs, heads, tq, tk, old_blocks, diag_in_old):
    if diag_in_old:
        q_ref, ko_ref, vo_ref, o_ref, acc_ref, carry_ref = refs
    else:
        q_ref, kd_ref, vd_ref, ko_ref, vo_ref, o_ref, acc_ref, carry_ref = refs
    qi = pl.program_id(1)

    def tri(n):
        r = lax.broadcasted_iota(jnp.int32, (n, n), 0)
        c = lax.broadcasted_iota(jnp.int32, (n, n), 1)
        return r, c

    r, c = tri(tq)
    mask = c < r
    upper_d = jnp.where(r > c, 1.0, 0.0).astype(BF16)
    ones_d = jnp.ones((tq, LANES), BF16)
    hs = range(heads)
    qs = [q_ref[0, h] for h in hs]
    if diag_in_old:
        rows = pl.ds(pl.multiple_of(qi * tq, tq), tq)
        kbs = [ko_ref[0, h, rows, :].astype(BF16) for h in hs]
        vbs = [vo_ref[0, h, rows, :].astype(BF16) for h in hs]
    else:
        kbs = [kd_ref[0, h].astype(BF16) for h in hs]
        vbs = [vd_ref[0, h].astype(BF16) for h in hs]
    outs, carries = _sb_block(qs, kbs, vbs, [None] * heads, upper_d, ones_d, mask)
    for h in hs:
        acc_ref[h] = outs[h]
        carry_ref[h] = carries[h]

    if old_blocks is None:
        j0 = qi - 1
    else:
        j0 = jnp.int32(old_blocks - 1)

    r, c = tri(tk)
    upper_o = jnp.where(r > c, 1.0, 0.0).astype(BF16)
    ones_o = jnp.ones((tk, LANES), BF16)

    def cond(state):
        j, live = state
        return jnp.logical_and(j >= 0, live > DEAD_LOG)

    def body(state):
        j, _ = state
        rows = pl.ds(pl.multiple_of(j * tk, tk), tk)
        qs = [q_ref[0, h] for h in hs]
        kbs = [ko_ref[0, h, rows, :].astype(BF16) for h in hs]
        vbs = [vo_ref[0, h, rows, :].astype(BF16) for h in hs]
        old = [carry_ref[h] for h in hs]
        accs = [acc_ref[h] for h in hs]
        outs, new = _sb_block(qs, kbs, vbs, old, upper_o, ones_o, None)
        for h in hs:
            acc_ref[h] = accs[h] + outs[h]
            carry_ref[h] = new[h]
        return j - 1, _max_all(new)

    lax.while_loop(cond, body, (j0, _max_all(carries)))
    for h in range(heads):
        o_ref[0, :, h * HEAD_DIM:(h + 1) * HEAD_DIM] = acc_ref[h].astype(BF16)


def _attention(q, k_new, v_new, k_old=None, v_old=None, layer=0):
    B, H, S, d = q.shape
    if k_old is None:
        tq = tk = min(S, ATT_BLOCK)
        assert S % tq == 0
        old_blocks, diag_in_old = None, True
        full = pl.BlockSpec((1, H, S, d), lambda b, i: (b, 0, 0, 0))
        in_specs = [pl.BlockSpec((1, H, tq, d), lambda b, i: (b, 0, i, 0)), full, full]
        args = (q, k_new, v_new)
    else:
        P = k_old.shape[3]
        tq, tk = S, min(P, ATT_BLOCK)
        assert P % tk == 0
        old_blocks, diag_in_old = P // tk, False
        new = pl.BlockSpec((1, H, S, d), lambda b, i: (b, 0, 0, 0))
        old = pl.BlockSpec((None, 1, H, P, d), lambda b, i: (layer, b, 0, 0, 0))
        in_specs = [new, new, new, old, old]
        args = (q, k_new, v_new, k_old, v_old)
    return pl.pallas_call(
        functools.partial(_attn_kernel, heads=H, tq=tq, tk=tk, old_blocks=old_blocks,
                          diag_in_old=diag_in_old),
        grid=(B, S // tq),
        in_specs=in_specs,
        out_specs=pl.BlockSpec((1, tq, H * d), lambda b, i: (b, i, 0)),
        out_shape=jax.ShapeDtypeStruct((B, S, H * d), BF16),
        scratch_shapes=[pltpu.VMEM((H, tq, d), F32), pltpu.VMEM((H, tq, LANES), F32)],
        compiler_params=_params(("parallel", "arbitrary")),
        name="attention",
    )(*args)


def _outproj_kernel(att_ref, c_ref, x_ref, wa_ref, wc_ref, g2_ref, x1_ref, xn_ref):
    x1 = (x_ref[...]
          + jnp.dot(att_ref[...], wa_ref[...], preferred_element_type=F32)
          + jnp.dot(c_ref[...], wc_ref[...], preferred_element_type=F32))
    x1_ref[...] = x1
    ms = jnp.mean(x1 * x1, axis=-1, keepdims=True)
    xn_ref[...] = ((x1 * lax.rsqrt(ms + EPS)) * g2_ref[...]).astype(BF16)


def _outproj(att, c, x, wa, wc, g2):
    N, D = x.shape
    T = min(N, ROW_TILE)
    assert N % T == 0
    row = lambda w: pl.BlockSpec((T, w), lambda i: (i, 0))
    return pl.pallas_call(
        _outproj_kernel,
        grid=(N // T,),
        in_specs=[row(att.shape[1]), row(c.shape[1]), row(D),
                  _const_spec(wa.shape), _const_spec(wc.shape), _const_spec(g2.shape)],
        out_specs=[row(D), row(D)],
        out_shape=[jax.ShapeDtypeStruct((N, D), F32), jax.ShapeDtypeStruct((N, D), BF16)],
        compiler_params=_params(("parallel",)),
        name="outproj",
    )(att, c, x, wa, wc, g2)


def _candidate_pairs():
    return [(p, q) for p in range(1, PEER_TOPK + 1) for q in range(1, PEER_TOPK + 1)
            if p * q <= PEER_TOPK]


def _sort_desc(v):
    v = list(v)
    n = len(v)
    k = 2
    while k <= n:
        j = k // 2
        while j >= 1:
            for i in range(n):
                m = i ^ j
                if m > i:
                    hi, lo = jnp.maximum(v[i], v[m]), jnp.minimum(v[i], v[m])
                    v[i], v[m] = (hi, lo) if (i & k) == 0 else (lo, hi)
            j //= 2
        k *= 2
    return v


def _merge_top(a, b):
    n = len(a)
    c = [jnp.maximum(a[i], b[n - 1 - i]) for i in range(n)]
    j = n // 2
    while j >= 1:
        for i in range(n):
            m = i ^ j
            if m > i:
                c[i], c[m] = jnp.maximum(c[i], c[m]), jnp.minimum(c[i], c[m])
        j //= 2
    return c


def _top_values(load, count):
    k = PEER_TOPK
    groups = []
    for g in range(0, count, k):
        vals = [load(n) for n in range(g, min(g + k, count))]
        vals += [jnp.full_like(vals[0], -jnp.inf)] * (k - len(vals))
        groups.append(_sort_desc(vals))
    while len(groups) > 1:
        merged = [_merge_top(groups[i], groups[i + 1]) for i in range(0, len(groups) - 1, 2)]
        groups = merged + ([groups[-1]] if len(groups) % 2 else [])
    return groups[0]


def _peer_route(xn_ref, wq_ref, kb_ref, xnt_ref, s_ref, top_ref, cand_ref, kth_ref,
                r0_ref, e0_ref, cnt_ref, e1_ref, *, nk, ph, tokens):
    xnt_ref[...] = xn_ref[...].astype(F32).T.astype(BF16)
    xnt = xnt_ref[...]
    for c in range(2):
        qt = jnp.dot(wq_ref[c], xnt, preferred_element_type=F32)
        sc = jnp.dot(kb_ref[c], qt.astype(BF16), preferred_element_type=F32)
        for lt in range(tokens // LANES):
            s_ref[c, lt] = sc[:, lt * LANES:(lt + 1) * LANES]

    pairs = _candidate_pairs()
    for lt in range(tokens // LANES):
        for c in range(2):
            top = _top_values(lambda n: s_ref[c, lt, n * ph:(n + 1) * ph, :], nk)
            for p in range(PEER_TOPK):
                top_ref[c, p, lt] = top[p]
        for n, (p, q) in enumerate(pairs):
            cand_ref[n, lt] = top_ref[0, p - 1, lt] + top_ref[1, q - 1, lt]
        kth_ref[lt] = _top_values(lambda n: cand_ref[n, lt], len(pairs))[PEER_TOPK - 1]
    theta = kth_ref[...]
    a1 = top_ref[0, 0]
    b1 = top_ref[1, 0]
    best = a1 + b1
    denom = None
    for n in range(len(pairs)):
        cv = cand_ref[n]
        t = jnp.where(cv >= theta, jnp.exp(cv - best), 0.0)
        denom = t if denom is None else denom + t
    inv = 1.0 / denom

    def first_half(i, _):
        rows = pl.ds(pl.multiple_of(i * ph, ph), ph)
        s0 = s_ref[0, :, rows, :]
        rank = jnp.ones_like(s0)
        for p in range(PEER_TOPK):
            rank = rank + jnp.where(top_ref[0, p] > s0, 1.0, 0.0)
        r0_ref[:, rows, :] = rank
        e0_ref[:, rows, :] = jnp.exp(s0 - a1) * inv
        return 0
    lax.fori_loop(0, nk, first_half, 0)

    def second_half(j, _):
        rows = pl.ds(pl.multiple_of(j * ph, ph), ph)
        s1 = s_ref[1, :, rows, :]
        cnt = jnp.zeros_like(s1)
        for p in range(PEER_TOPK):
            cnt = cnt + jnp.where(top_ref[0, p] + s1 >= theta, 1.0, 0.0)
        s_ref[1, :, rows, :] = cnt
        s_ref[0, :, rows, :] = jnp.exp(s1 - b1)
        return 0
    lax.fori_loop(0, nk, second_half, 0)
    for lt in range(tokens // LANES):
        for h in range(ph):
            cnt_ref[lt, h * nk:(h + 1) * nk, :] = (
                s_ref[1, lt, pl.ds(h, nk, stride=ph), :].astype(BF16))
            e1_ref[lt, h * nk:(h + 1) * nk, :] = (
                s_ref[0, lt, pl.ds(h, nk, stride=ph), :].astype(BF16))


def _zero_after(x):
    bits = pltpu.bitcast(x, jnp.uint32)
    bits = lax.shift_right_logical(lax.shift_right_logical(bits, jnp.uint32(16)), jnp.uint32(16))
    return pltpu.bitcast(bits, F32)


def _peer_gate_chunk(c, xu, coef_ref, r0_ref, e0_ref, cnt_ref, e1_ref, *, nk, ph, chunk, tokens):
    tile = (PEER_SUB, LANES)

    for ib in range(chunk // nk):
        i = c * (chunk // nk) + ib
        hrows = pl.ds(pl.multiple_of(i * ph, ph), ph)
        row0 = ib * nk
        for lt in range(tokens // LANES):
            lanes = slice(lt * LANES, (lt + 1) * LANES)
            zero = _zero_after(xu[row0:row0 + SUBLANES, lanes])
            rank = r0_ref[lt, hrows, :] + zero
            e0 = e0_ref[lt, hrows, :] + zero
            w = [None] * (nk // PEER_SUB)
            for h in range(ph):
                rank_b = jnp.broadcast_to(rank[h:h + 1], tile).astype(BF16)
                e0_b = jnp.broadcast_to(e0[h:h + 1], tile).astype(BF16)
                for jb in range(nk // PEER_SUB):
                    jrows = slice(h * nk + jb * PEER_SUB, h * nk + (jb + 1) * PEER_SUB)
                    t = jnp.where(rank_b <= cnt_ref[lt, jrows, :],
                                  e1_ref[lt, jrows, :], 0.0) * e0_b
                    w[jb] = t if w[jb] is None else w[jb] + t
            for jb in range(nk // PEER_SUB):
                erows = slice(row0 + jb * PEER_SUB, row0 + (jb + 1) * PEER_SUB)
                xs = xu[erows, lanes]
                act = 0.5 * xs * (1.0 + lax.erf(xs * math.sqrt(0.5)))
                coef_ref[erows, lanes] = w[jb] * act.astype(BF16)


def _peer_kernel(xn_ref, x1_ref, wq_ref, kb_ref, u_ref, vt_ref, y_ref,
                 s_ref, top_ref, cand_ref, kth_ref, r0_ref, e0_ref, cnt_ref, e1_ref,
                 acc_ref, coef_ref, xnt_ref, *, nk, ph, chunk, tokens):
    e = pl.program_id(1)

    @pl.when(e == 0)
    def _():
        _peer_route(xn_ref, wq_ref, kb_ref, xnt_ref, s_ref, top_ref, cand_ref, kth_ref,
                    r0_ref, e0_ref, cnt_ref, e1_ref, nk=nk, ph=ph, tokens=tokens)
        acc_ref[...] = jnp.zeros_like(acc_ref)

    xu = jnp.dot(u_ref[...], xnt_ref[...], preferred_element_type=F32)
    _peer_gate_chunk(e, xu, coef_ref, r0_ref, e0_ref, cnt_ref, e1_ref,
                     nk=nk, ph=ph, chunk=chunk, tokens=tokens)
    acc_ref[...] += jnp.dot(vt_ref[...], coef_ref[...], preferred_element_type=F32)

    @pl.when(e == pl.num_programs(1) - 1)
    def _():
        y_ref[...] = x1_ref[...] + acc_ref[...].T


def _peer(xn, x1, wq_t, kbig, u_bf, vt_bf):
    N, D = xn.shape
    E = u_bf.shape[0]
    ph = PEER_HEADS
    nk = kbig.shape[1] // ph
    T = min(N, PEER_TOKENS)
    chunk = min(E, PEER_EXPERT_CHUNK)
    assert N % T == 0 and E % chunk == 0 and chunk % nk == 0 and T % LANES == 0
    npairs = len(_candidate_pairs())
    lt = T // LANES
    once = dict(pipeline_mode=pl.Buffered(1))
    return pl.pallas_call(
        functools.partial(_peer_kernel, nk=nk, ph=ph, chunk=chunk, tokens=T),
        grid=(N // T, E // chunk),
        in_specs=[pl.BlockSpec((T, D), lambda t, e: (t, 0)),
                  pl.BlockSpec((T, D), lambda t, e: (t, 0), **once),
                  pl.BlockSpec(wq_t.shape, lambda t, e: (0, 0, 0), **once),
                  pl.BlockSpec(kbig.shape, lambda t, e: (0, 0, 0), **once),
                  pl.BlockSpec((chunk, D), lambda t, e: (e, 0)),
                  pl.BlockSpec((D, chunk), lambda t, e: (0, e))],
        out_specs=pl.BlockSpec((T, D), lambda t, e: (t, 0)),
        out_shape=jax.ShapeDtypeStruct((N, D), F32),
        scratch_shapes=[pltpu.VMEM((2, lt, nk * ph, LANES), F32),
                        pltpu.VMEM((2, PEER_TOPK, lt, ph, LANES), F32),
                        pltpu.VMEM((npairs, lt, ph, LANES), F32),
                        pltpu.VMEM((lt, ph, LANES), F32),
                        pltpu.VMEM((lt, nk * ph, LANES), F32),
                        pltpu.VMEM((lt, nk * ph, LANES), F32),
                        pltpu.VMEM((lt, ph * nk, LANES), BF16),
                        pltpu.VMEM((lt, ph * nk, LANES), BF16),
                        pltpu.VMEM((D, T), F32),
                        pltpu.VMEM((chunk, T), BF16),
                        pltpu.VMEM((D, T), BF16)],
        compiler_params=_params(("parallel", "arbitrary")),
        name="peer",
    )(xn, x1, wq_t, kbig, u_bf, vt_bf)


def _layer_weights(norm1_g, w_in, q_norm_g, k_norm_g, conv_w, conv_b, conv_ln_g, conv_ln_b,
                   w_out, norm2_g, peer_wq, peer_keys, peer_u, peer_v):
    att_w = (w_in.shape[1] - 2 * conv_w.shape[1]) // 3
    heads = att_w // HEAD_DIM
    row = lambda v: v.reshape(1, -1)
    head_of = jnp.arange(att_w) // HEAD_DIM
    gm = jnp.where(head_of[:, None] == head_of[None, :], 1.0 / HEAD_DIM, 0.0).astype(BF16)
    _, nk, half = peer_keys.shape
    ph = PEER_HEADS
    D = peer_wq.shape[0]
    wq_t = peer_wq.reshape(D, ph, 2, half).transpose(2, 1, 3, 0).reshape(2, ph * half, D)
    eye = jnp.eye(ph, dtype=peer_keys.dtype)
    kbig = (peer_keys[:, :, None, None, :] * eye[None, None, :, :, None]
            ).reshape(2, nk * ph, ph * half)
    return dict(
        g1=row(norm1_g), w_in=w_in.astype(BF16),
        gq=row(jnp.tile(q_norm_g, heads)), gk=row(jnp.tile(k_norm_g, heads)), gm=gm,
        cw=conv_w, cb=row(conv_b), lg=row(conv_ln_g), lb=row(conv_ln_b),
        wa=w_out[:att_w].astype(BF16), wc=w_out[att_w:].astype(BF16), g2=row(norm2_g),
        wq_t=wq_t.astype(BF16), kbig=kbig.astype(BF16),
        u=peer_u.astype(BF16), vt=peer_v.T.astype(BF16))


def _layer(x, caches, conv_hist, w, layer):
    B, S, D = x.shape
    q, k, v, glu = _inproj(x, w["g1"], w["w_in"], w["gq"], w["gk"], w["gm"])
    if caches is None:
        att = _attention(q, k, v)
        conv_hist = jnp.zeros((B, w["cw"].shape[0] - 1, glu.shape[-1]), x.dtype)
    else:
        att = _attention(q, k, v, *caches, layer=layer)
    c, new_conv = _conv(glu, conv_hist, w["cw"], w["cb"], w["lg"], w["lb"])
    x1, xn = _outproj(att.reshape(B * S, -1), c.reshape(B * S, -1), x.reshape(B * S, D),
                      w["wa"], w["wc"], w["g2"])
    y = _peer(xn, x1, w["wq_t"], w["kbig"], w["u"], w["vt"])
    return y.reshape(B, S, D), k, v, new_conv


def kernel(x_prompt, x_sample, cache_k, cache_v, state_conv, norm1_g, w_in, q_norm_g, k_norm_g,
           conv_w, conv_b, conv_ln_g, conv_ln_b, w_out, norm2_g, peer_wq, peer_keys, peer_u,
           peer_v):
    y_p, y_s = x_prompt, x_sample
    outs = [[] for _ in range(6)]
    for l in range(norm1_g.shape[0]):
        w = _layer_weights(norm1_g[l], w_in[l], q_norm_g[l], k_norm_g[l], conv_w[l], conv_b[l],
                           conv_ln_g[l], conv_ln_b[l], w_out[l], norm2_g[l], peer_wq[l],
                           peer_keys[l], peer_u[l], peer_v[l])
        y_p, kp, vp, cp = _layer(y_p, None, None, w, l)
        y_s, ks, vs, cs = _layer(y_s, (cache_k, cache_v), state_conv[l], w, l)
        for lst, val in zip(outs, (kp, vp, cp, ks, vs, cs)):
            lst.append(val)
    return (y_p, y_s) + tuple(jnp.stack(o) if len(o) > 1 else o[0][None] for o in outs)
```

```python
import functools
import math

import jax
import jax.numpy as jnp
from jax import lax
from jax.experimental import pallas as pl
from jax.experimental.pallas import tpu as pltpu

F32 = jnp.float32
BF16 = jnp.bfloat16

EPS = 1e-6
HEAD_DIM = 64
PEER_HEADS = 8
PEER_TOPK = 16
LANES = 128
SUBLANES = 8
ROW_TILE = 512
CONV_ROWS = 64
ATT_BLOCK = 128
PEER_TOKENS = 512
PEER_EXPERT_CHUNK = 1024
PEER_SUB = 16
VMEM_LIMIT = 60 * 1024 * 1024
DEAD_LOG = -120.0

_NT = (((1,), (1,)), ((), ()))


def _const_spec(shape):
    zeros = (0,) * len(shape)
    return pl.BlockSpec(shape, lambda *_: zeros)


def _params(semantics):
    return pltpu.CompilerParams(dimension_semantics=semantics, vmem_limit_bytes=VMEM_LIMIT)


def _inproj_kernel(x_ref, g1_ref, w_ref, gq_ref, gk_ref, gm_ref, q_ref, *out_refs,
                   heads, att_w, conv_ch, transposed):
    x = x_ref[0]
    ms = jnp.mean(x * x, axis=-1, keepdims=True)
    hb = ((x * lax.rsqrt(ms + EPS)) * g1_ref[...]).astype(BF16)

    def proj(c0, width):
        return jnp.dot(hb, w_ref[:, c0:c0 + width], preferred_element_type=F32)

    def head_norm(t, g):
        tt = t * t
        hi = tt.astype(BF16)
        lo = (tt - hi.astype(F32)).astype(BF16)
        msq = (jnp.dot(hi, gm_ref[...], preferred_element_type=F32)
               + jnp.dot(lo, gm_ref[...], preferred_element_type=F32))
        return (t * lax.rsqrt(msq + EPS)) * g

    qn = head_norm(proj(0, att_w), gq_ref[...]) * (HEAD_DIM ** -0.5)
    kn = head_norm(proj(att_w, att_w), gk_ref[...])
    v = proj(2 * att_w, att_w)
    if transposed:
        kt_ref, vt_ref, ktb_ref, vb_ref, glu_ref = out_refs
        knt, vt = kn.T, v.T
    else:
        k_ref, v_ref, glu_ref = out_refs
    for h in range(heads):
        sl = slice(h * HEAD_DIM, (h + 1) * HEAD_DIM)
        q_ref[0, h] = qn[:, sl].astype(BF16)
        if transposed:
            kt_ref[0, h] = knt[sl, :]
            vt_ref[0, h] = vt[sl, :]
            ktb_ref[0, h] = knt[sl, :].astype(BF16)
            vb_ref[0, h] = v[:, sl].astype(BF16)
        else:
            k_ref[0, h] = kn[:, sl]
            v_ref[0, h] = v[:, sl]
    a = proj(3 * att_w, conv_ch)
    gate = proj(3 * att_w + conv_ch, conv_ch)
    glu_ref[0] = a * jax.nn.sigmoid(gate)


def _inproj(x, g1, w_bf, gq, gk, gm, transposed):
    B, S, D = x.shape
    att_w = gq.shape[-1]
    heads = att_w // HEAD_DIM
    conv_ch = (w_bf.shape[1] - 3 * att_w) // 2
    T = min(S, ROW_TILE)
    assert S % T == 0
    hm = lambda dt: jax.ShapeDtypeStruct((B, heads, S, HEAD_DIM), dt)
    hspec = pl.BlockSpec((1, heads, T, HEAD_DIM), lambda b, s: (b, 0, s, 0))
    tm = lambda dt: jax.ShapeDtypeStruct((B, heads, HEAD_DIM, S), dt)
    tspec = pl.BlockSpec((1, heads, HEAD_DIM, T), lambda b, s: (b, 0, 0, s))
    if transposed:
        kv_specs, kv_shapes = [tspec, tspec, tspec, hspec], [tm(F32), tm(F32), tm(BF16), hm(BF16)]
    else:
        kv_specs, kv_shapes = [hspec, hspec], [hm(F32), hm(F32)]
    return pl.pallas_call(
        functools.partial(_inproj_kernel, heads=heads, att_w=att_w, conv_ch=conv_ch,
                          transposed=transposed),
        grid=(B, S // T),
        in_specs=[pl.BlockSpec((1, T, D), lambda b, s: (b, s, 0)),
                  _const_spec(g1.shape), _const_spec(w_bf.shape),
                  _const_spec(gq.shape), _const_spec(gk.shape), _const_spec(gm.shape)],
        out_specs=[hspec] + kv_specs + [pl.BlockSpec((1, T, conv_ch), lambda b, s: (b, s, 0))],
        out_shape=[hm(BF16)] + kv_shapes + [jax.ShapeDtypeStruct((B, S, conv_ch), F32)],
        compiler_params=_params(("parallel", "parallel")),
        name="inproj",
    )(x, g1, w_bf, gq, gk, gm)


def _conv_kernel(glu_ref, hist_ref, w_ref, b_ref, lg_ref, lb_ref, c_ref, new_ref, ext_ref,
                 *, rows, width, chunk):
    hist = width - 1
    base = 32
    off = base - hist
    s = pl.program_id(1)

    @pl.when(s == 0)
    def _():
        ext_ref[off:base, :] = hist_ref[0]
        ext_ref[base + rows:, :] = jnp.zeros((SUBLANES, ext_ref.shape[1]), F32)

    ext_ref[base:base + rows, :] = glu_ref[0]
    span = chunk + SUBLANES
    lanes_all = ext_ref.shape[1]
    for r0 in range(0, rows, chunk):
        cols = []
        for l0 in range(0, lanes_all, LANES):
            acc = None
            for b in range(SUBLANES):
                part = None
                for a in range((off + width - 1) // SUBLANES + 1):
                    w = SUBLANES * a + b - off
                    if 0 <= w < width:
                        term = (ext_ref[r0 + SUBLANES * a:r0 + SUBLANES * a + span, l0:l0 + LANES]
                                * w_ref[w:w + 1, l0:l0 + LANES])
                        part = term if part is None else part + term
                if part is not None:
                    acc = part[b:b + chunk] if acc is None else acc + part[b:b + chunk]
            cols.append(acc)
        acc = jnp.concatenate(cols, axis=1) + b_ref[...]
        mu = jnp.mean(acc, axis=-1, keepdims=True)
        d = acc - mu
        var = jnp.mean(d * d, axis=-1, keepdims=True)
        y = d * lax.rsqrt(var + EPS) * lg_ref[...] + lb_ref[...]
        c_ref[0, r0:r0 + chunk, :] = (y * jax.nn.sigmoid(y)).astype(BF16)
    tail = ext_ref[rows + off:rows + base, :]
    new_ref[0] = tail
    ext_ref[off:base, :] = tail


def _conv(glu, hist, cw, cb, lg, lb):
    B, S, C = glu.shape
    width = cw.shape[0]
    T = min(S, ROW_TILE)
    assert S % T == 0 and (T >= width - 1 or S == T)
    chunk = min(T, CONV_ROWS)
    return pl.pallas_call(
        functools.partial(_conv_kernel, rows=T, width=width, chunk=chunk),
        grid=(B, S // T),
        in_specs=[pl.BlockSpec((1, T, C), lambda b, s: (b, s, 0)),
                  pl.BlockSpec((1, width - 1, C), lambda b, s: (b, 0, 0)),
                  _const_spec(cw.shape), _const_spec(cb.shape),
                  _const_spec(lg.shape), _const_spec(lb.shape)],
        out_specs=[pl.BlockSpec((1, T, C), lambda b, s: (b, s, 0)),
                   pl.BlockSpec((1, width - 1, C), lambda b, s: (b, 0, 0))],
        out_shape=[jax.ShapeDtypeStruct((B, S, C), BF16),
                   jax.ShapeDtypeStruct((B, width - 1, C), F32)],
        scratch_shapes=[pltpu.VMEM((32 + T + SUBLANES, C), F32)],
        compiler_params=_params(("parallel", "arbitrary")),
        name="conv",
    )(glu, hist, cw, cb, lg, lb)


def _sb_block(qs, kts, vbs, carries, upper, ones, mask, v_transposed):
    tq, tk = qs[0].shape[0], kts[0].shape[1]
    zs = [jnp.dot(q, kt, preferred_element_type=F32) for q, kt in zip(qs, kts)]
    log_betas, sums = [], []
    for z in zs:
        sp = jnp.log1p(jnp.exp(-jnp.abs(z)))
        log_beta = jnp.minimum(z, 0.0) - sp
        log_1m = log_beta - z
        if mask is not None:
            log_1m = jnp.where(mask, log_1m, 0.0)
        hi = log_1m.astype(BF16)
        lo = (log_1m - hi.astype(F32)).astype(BF16)
        both = jnp.concatenate([hi, lo], axis=0)
        if tk == LANES:
            p = jnp.dot(both, jnp.concatenate([upper, ones], axis=1),
                        preferred_element_type=F32)
            after, total = p[:, :tk], p[:, tk:]
        else:
            after = jnp.dot(both, upper, preferred_element_type=F32)
            total = jnp.dot(both, ones, preferred_element_type=F32)
        log_betas.append(log_beta)
        sums.append((after[:tq] + after[tq:], total[:tq] + total[tq:]))
    outs, new_carries = [], []
    for log_beta, (after, total), carry, vb in zip(log_betas, sums, carries, vbs):
        logw = log_beta + after
        if carry is not None:
            logw = logw + (carry if tk == LANES else carry[:, :1])
        w = jnp.exp(logw)
        if mask is not None:
            w = jnp.where(mask, w, 0.0)
        if v_transposed:
            outs.append(lax.dot_general(w.astype(BF16), vb, _NT, preferred_element_type=F32))
        else:
            outs.append(jnp.dot(w.astype(BF16), vb, preferred_element_type=F32))
        new_carries.append(total if carry is None else carry + total)
    return outs, new_carries


def _max_all(xs):
    m = xs[0]
    for x in xs[1:]:
        m = jnp.maximum(m, x)
    return jnp.max(m)


def _attn_kernel(*refs, heads, tq, tk, old_blocks, diag_in_old):
    if diag_in_old:
        q_ref, ko_ref, vo_ref, o_ref, acc_ref, carry_ref = refs
    else:
        q_ref, kd_ref, vd_ref, ko_ref, vo_ref, o_ref, acc_ref, carry_ref = refs
    qi = pl.program_id(1)

    def tri(n):
        r = lax.broadcasted_iota(jnp.int32, (n, n), 0)
        c = lax.broadcasted_iota(jnp.int32, (n, n), 1)
        return r, c

    r, c = tri(tq)
    mask = c < r
    upper_d = jnp.where(r > c, 1.0, 0.0).astype(BF16)
    ones_d = jnp.ones((tq, LANES), BF16)
    hs = range(heads)
    qs = [q_ref[0, h] for h in hs]
    if diag_in_old:
        rows = pl.ds(pl.multiple_of(qi * tq, tq), tq)
        kts = [ko_ref[0, h, :, rows] for h in hs]
        vbs = [vo_ref[0, h, rows, :] for h in hs]
    else:
        kts = [kd_ref[0, h] for h in hs]
        vbs = [vd_ref[0, h] for h in hs]
    outs, carries = _sb_block(qs, kts, vbs, [None] * heads, upper_d, ones_d, mask, False)
    for h in hs:
        acc_ref[h] = outs[h]
        carry_ref[h] = carries[h]

    if old_blocks is None:
        j0 = qi - 1
    else:
        j0 = jnp.int32(old_blocks - 1)

    r, c = tri(tk)
    upper_o = jnp.where(r > c, 1.0, 0.0).astype(BF16)
    ones_o = jnp.ones((tk, LANES), BF16)

    def cond(state):
        j, live = state
        return jnp.logical_and(j >= 0, live > DEAD_LOG)

    def body(state):
        j, _ = state
        rows = pl.ds(pl.multiple_of(j * tk, tk), tk)
        qs = [q_ref[0, h] for h in hs]
        kts = [ko_ref[0, h, :, rows].astype(BF16) for h in hs]
        if diag_in_old:
            vbs = [vo_ref[0, h, rows, :] for h in hs]
        else:
            vbs = [vo_ref[0, h, :, rows].astype(BF16) for h in hs]
        old = [carry_ref[h] for h in hs]
        accs = [acc_ref[h] for h in hs]
        outs, new = _sb_block(qs, kts, vbs, old, upper_o, ones_o, None, not diag_in_old)
        for h in hs:
            acc_ref[h] = accs[h] + outs[h]
            carry_ref[h] = new[h]
        return j - 1, _max_all(new)

    lax.while_loop(cond, body, (j0, _max_all(carries)))
    for h in range(heads):
        o_ref[0, :, h * HEAD_DIM:(h + 1) * HEAD_DIM] = acc_ref[h].astype(BF16)


def _attention(q, kt_new, v_new, kt_old=None, vt_old=None, layer=0):
    B, H, S, d = q.shape
    if kt_old is None:
        tq = tk = min(S, ATT_BLOCK)
        assert S % tq == 0
        old_blocks, diag_in_old = None, True
        in_specs = [pl.BlockSpec((1, H, tq, d), lambda b, i: (b, 0, i, 0)),
                    pl.BlockSpec((1, H, d, S), lambda b, i: (b, 0, 0, 0)),
                    pl.BlockSpec((1, H, S, d), lambda b, i: (b, 0, 0, 0))]
        args = (q, kt_new, v_new)
    else:
        P = kt_old.shape[4]
        tq, tk = S, min(P, ATT_BLOCK)
        assert P % tk == 0
        old_blocks, diag_in_old = P // tk, False
        new = pl.BlockSpec((1, H, S, d), lambda b, i: (b, 0, 0, 0))
        new_t = pl.BlockSpec((1, H, d, S), lambda b, i: (b, 0, 0, 0))
        old = pl.BlockSpec((None, 1, H, d, P), lambda b, i: (layer, b, 0, 0, 0))
        in_specs = [new, new_t, new, old, old]
        args = (q, kt_new, v_new, kt_old, vt_old)
    return pl.pallas_call(
        functools.partial(_attn_kernel, heads=H, tq=tq, tk=tk, old_blocks=old_blocks,
                          diag_in_old=diag_in_old),
        grid=(B, S // tq),
        in_specs=in_specs,
        out_specs=pl.BlockSpec((1, tq, H * d), lambda b, i: (b, i, 0)),
        out_shape=jax.ShapeDtypeStruct((B, S, H * d), BF16),
        scratch_shapes=[pltpu.VMEM((H, tq, d), F32), pltpu.VMEM((H, tq, LANES), F32)],
        compiler_params=_params(("parallel", "arbitrary")),
        name="attention",
    )(*args)


def _outproj_kernel(att_ref, c_ref, x_ref, wa_ref, wc_ref, g2_ref, x1_ref, xn_ref):
    x1 = (x_ref[...]
          + jnp.dot(att_ref[...], wa_ref[...], preferred_element_type=F32)
          + jnp.dot(c_ref[...], wc_ref[...], preferred_element_type=F32))
    x1_ref[...] = x1
    ms = jnp.mean(x1 * x1, axis=-1, keepdims=True)
    xn_ref[...] = ((x1 * lax.rsqrt(ms + EPS)) * g2_ref[...]).astype(BF16)


def _outproj(att, c, x, wa, wc, g2):
    N, D = x.shape
    T = min(N, ROW_TILE)
    assert N % T == 0
    row = lambda w: pl.BlockSpec((T, w), lambda i: (i, 0))
    return pl.pallas_call(
        _outproj_kernel,
        grid=(N // T,),
        in_specs=[row(att.shape[1]), row(c.shape[1]), row(D),
                  _const_spec(wa.shape), _const_spec(wc.shape), _const_spec(g2.shape)],
        out_specs=[row(D), row(D)],
        out_shape=[jax.ShapeDtypeStruct((N, D), F32), jax.ShapeDtypeStruct((N, D), BF16)],
        compiler_params=_params(("parallel",)),
        name="outproj",
    )(att, c, x, wa, wc, g2)


def _candidate_pairs():
    return [(p, q) for p in range(1, PEER_TOPK + 1) for q in range(1, PEER_TOPK + 1)
            if p * q <= PEER_TOPK]


def _sort_desc(v):
    v = list(v)
    n = len(v)
    k = 2
    while k <= n:
        j = k // 2
        while j >= 1:
            for i in range(n):
                m = i ^ j
                if m > i:
                    hi, lo = jnp.maximum(v[i], v[m]), jnp.minimum(v[i], v[m])
                    v[i], v[m] = (hi, lo) if (i & k) == 0 else (lo, hi)
            j //= 2
        k *= 2
    return v


def _merge_top(a, b):
    n = len(a)
    c = [jnp.maximum(a[i], b[n - 1 - i]) for i in range(n)]
    j = n // 2
    while j >= 1:
        for i in range(n):
            m = i ^ j
            if m > i:
                c[i], c[m] = jnp.maximum(c[i], c[m]), jnp.minimum(c[i], c[m])
        j //= 2
    return c


def _top_values(load, count):
    k = PEER_TOPK
    groups = []
    for g in range(0, count, k):
        vals = [load(n) for n in range(g, min(g + k, count))]
        vals += [jnp.full_like(vals[0], -jnp.inf)] * (k - len(vals))
        groups.append(_sort_desc(vals))
    while len(groups) > 1:
        merged = [_merge_top(groups[i], groups[i + 1]) for i in range(0, len(groups) - 1, 2)]
        groups = merged + ([groups[-1]] if len(groups) % 2 else [])
    return groups[0]


def _peer_route(xn_ref, wq_ref, kb_ref, xnt_ref, s_ref, top_ref, cand_ref, kth_ref,
                r0_ref, e0_ref, cnt_ref, e1_ref, *, nk, ph, tokens):
    xnt_ref[...] = xn_ref[...].astype(F32).T.astype(BF16)
    xnt = xnt_ref[...]
    for c in range(2):
        qt = jnp.dot(wq_ref[c], xnt, preferred_element_type=F32)
        sc = jnp.dot(kb_ref[c], qt.astype(BF16), preferred_element_type=F32)
        for lt in range(tokens // LANES):
            s_ref[c, lt] = sc[:, lt * LANES:(lt + 1) * LANES]

    pairs = _candidate_pairs()
    for lt in range(tokens // LANES):
        for c in range(2):
            top = _top_values(lambda n: s_ref[c, lt, n * ph:(n + 1) * ph, :], nk)
            for p in range(PEER_TOPK):
                top_ref[c, p, lt] = top[p]
        for n, (p, q) in enumerate(pairs):
            cand_ref[n, lt] = top_ref[0, p - 1, lt] + top_ref[1, q - 1, lt]
        kth_ref[lt] = _top_values(lambda n: cand_ref[n, lt], len(pairs))[PEER_TOPK - 1]
    theta = kth_ref[...]
    a1 = top_ref[0, 0]
    b1 = top_ref[1, 0]
    best = a1 + b1
    denom = None
    for n in range(len(pairs)):
        cv = cand_ref[n]
        t = jnp.where(cv >= theta, jnp.exp(cv - best), 0.0)
        denom = t if denom is None else denom + t
    inv = 1.0 / denom

    def first_half(i, _):
        rows = pl.ds(pl.multiple_of(i * ph, ph), ph)
        s0 = s_ref[0, :, rows, :]
        rank = jnp.ones_like(s0)
        for p in range(PEER_TOPK):
            rank = rank + jnp.where(top_ref[0, p] > s0, 1.0, 0.0)
        r0_ref[:, rows, :] = rank
        e0_ref[:, rows, :] = jnp.exp(s0 - a1) * inv
        return 0
    lax.fori_loop(0, nk, first_half, 0)

    def second_half(j, _):
        rows = pl.ds(pl.multiple_of(j * ph, ph), ph)
        s1 = s_ref[1, :, rows, :]
        cnt = jnp.zeros_like(s1)
        for p in range(PEER_TOPK):
            cnt = cnt + jnp.where(top_ref[0, p] + s1 >= theta, 1.0, 0.0)
        s_ref[1, :, rows, :] = cnt
        s_ref[0, :, rows, :] = jnp.exp(s1 - b1)
        return 0
    lax.fori_loop(0, nk, second_half, 0)
    for lt in range(tokens // LANES):
        for h in range(ph):
            cnt_ref[lt, h * nk:(h + 1) * nk, :] = (
                s_ref[1, lt, pl.ds(h, nk, stride=ph), :].astype(BF16))
            e1_ref[lt, h * nk:(h + 1) * nk, :] = (
                s_ref[0, lt, pl.ds(h, nk, stride=ph), :].astype(BF16))


def _zero_after(x):
    bits = pltpu.bitcast(x, jnp.uint32)
    bits = lax.shift_right_logical(lax.shift_right_logical(bits, jnp.uint32(16)), jnp.uint32(16))
    return pltpu.bitcast(bits, F32)


def _peer_gate_chunk(c, xu, coef_ref, r0_ref, e0_ref, cnt_ref, e1_ref, *, nk, ph, chunk, tokens):
    tile = (PEER_SUB, LANES)

    for ib in range(chunk // nk):
        i = c * (chunk // nk) + ib
        hrows = pl.ds(pl.multiple_of(i * ph, ph), ph)
        row0 = ib * nk
        for lt in range(tokens // LANES):
            lanes = slice(lt * LANES, (lt + 1) * LANES)
            zero = _zero_after(xu[row0:row0 + SUBLANES, lanes])
            rank = r0_ref[lt, hrows, :] + zero
            e0 = e0_ref[lt, hrows, :] + zero
            w = [None] * (nk // PEER_SUB)
            for h in range(ph):
                rank_b = jnp.broadcast_to(rank[h:h + 1], tile).astype(BF16)
                e0_b = jnp.broadcast_to(e0[h:h + 1], tile).astype(BF16)
                for jb in range(nk // PEER_SUB):
                    jrows = slice(h * nk + jb * PEER_SUB, h * nk + (jb + 1) * PEER_SUB)
                    t = jnp.where(rank_b <= cnt_ref[lt, jrows, :],
                                  e1_ref[lt, jrows, :], 0.0) * e0_b
                    w[jb] = t if w[jb] is None else w[jb] + t
            for jb in range(nk // PEER_SUB):
                erows = slice(row0 + jb * PEER_SUB, row0 + (jb + 1) * PEER_SUB)
                xs = xu[erows, lanes]
                act = 0.5 * xs * (1.0 + lax.erf(xs * math.sqrt(0.5)))
                coef_ref[erows, lanes] = w[jb] * act.astype(BF16)


def _peer_kernel(xn_ref, x1_ref, wq_ref, kb_ref, u_ref, vt_ref, y_ref,
                 s_ref, top_ref, cand_ref, kth_ref, r0_ref, e0_ref, cnt_ref, e1_ref,
                 acc_ref, coef_ref, xnt_ref, *, nk, ph, chunk, tokens):
    e = pl.program_id(1)

    @pl.when(e == 0)
    def _():
        _peer_route(xn_ref, wq_ref, kb_ref, xnt_ref, s_ref, top_ref, cand_ref, kth_ref,
                    r0_ref, e0_ref, cnt_ref, e1_ref, nk=nk, ph=ph, tokens=tokens)
        acc_ref[...] = jnp.zeros_like(acc_ref)

    xu = jnp.dot(u_ref[...], xnt_ref[...], preferred_element_type=F32)
    _peer_gate_chunk(e, xu, coef_ref, r0_ref, e0_ref, cnt_ref, e1_ref,
                     nk=nk, ph=ph, chunk=chunk, tokens=tokens)
    acc_ref[...] += jnp.dot(vt_ref[...], coef_ref[...], preferred_element_type=F32)

    @pl.when(e == pl.num_programs(1) - 1)
    def _():
        y_ref[...] = x1_ref[...] + acc_ref[...].T


def _peer(xn, x1, wq_t, kbig, u_bf, vt_bf):
    N, D = xn.shape
    E = u_bf.shape[0]
    ph = PEER_HEADS
    nk = kbig.shape[1] // ph
    T = min(N, PEER_TOKENS)
    chunk = min(E, PEER_EXPERT_CHUNK)
    assert N % T == 0 and E % chunk == 0 and chunk % nk == 0 and T % LANES == 0
    npairs = len(_candidate_pairs())
    lt = T // LANES
    once = dict(pipeline_mode=pl.Buffered(1))
    return pl.pallas_call(
        functools.partial(_peer_kernel, nk=nk, ph=ph, chunk=chunk, tokens=T),
        grid=(N // T, E // chunk),
        in_specs=[pl.BlockSpec((T, D), lambda t, e: (t, 0)),
                  pl.BlockSpec((T, D), lambda t, e: (t, 0), **once),
                  pl.BlockSpec(wq_t.shape, lambda t, e: (0, 0, 0), **once),
                  pl.BlockSpec(kbig.shape, lambda t, e: (0, 0, 0), **once),
                  pl.BlockSpec((chunk, D), lambda t, e: (e, 0)),
                  pl.BlockSpec((D, chunk), lambda t, e: (0, e))],
        out_specs=pl.BlockSpec((T, D), lambda t, e: (t, 0)),
        out_shape=jax.ShapeDtypeStruct((N, D), F32),
        scratch_shapes=[pltpu.VMEM((2, lt, nk * ph, LANES), F32),
                        pltpu.VMEM((2, PEER_TOPK, lt, ph, LANES), F32),
                        pltpu.VMEM((npairs, lt, ph, LANES), F32),
                        pltpu.VMEM((lt, ph, LANES), F32),
                        pltpu.VMEM((lt, nk * ph, LANES), F32),
                        pltpu.VMEM((lt, nk * ph, LANES), F32),
                        pltpu.VMEM((lt, ph * nk, LANES), BF16),
                        pltpu.VMEM((lt, ph * nk, LANES), BF16),
                        pltpu.VMEM((D, T), F32),
                        pltpu.VMEM((chunk, T), BF16),
                        pltpu.VMEM((D, T), BF16)],
        compiler_params=_params(("parallel", "arbitrary")),
        name="peer",
    )(xn, x1, wq_t, kbig, u_bf, vt_bf)


def _layer_weights(norm1_g, w_in, q_norm_g, k_norm_g, conv_w, conv_b, conv_ln_g, conv_ln_b,
                   w_out, norm2_g, peer_wq, peer_keys, peer_u, peer_v):
    att_w = (w_in.shape[1] - 2 * conv_w.shape[1]) // 3
    heads = att_w // HEAD_DIM
    row = lambda v: v.reshape(1, -1)
    head_of = jnp.arange(att_w) // HEAD_DIM
    gm = jnp.where(head_of[:, None] == head_of[None, :], 1.0 / HEAD_DIM, 0.0).astype(BF16)
    _, nk, half = peer_keys.shape
    ph = PEER_HEADS
    D = peer_wq.shape[0]
    wq_t = peer_wq.reshape(D, ph, 2, half).transpose(2, 1, 3, 0).reshape(2, ph * half, D)
    eye = jnp.eye(ph, dtype=peer_keys.dtype)
    kbig = (peer_keys[:, :, None, None, :] * eye[None, None, :, :, None]
            ).reshape(2, nk * ph, ph * half)
    return dict(
        g1=row(norm1_g), w_in=w_in.astype(BF16),
        gq=row(jnp.tile(q_norm_g, heads)), gk=row(jnp.tile(k_norm_g, heads)), gm=gm,
        cw=conv_w, cb=row(conv_b), lg=row(conv_ln_g), lb=row(conv_ln_b),
        wa=w_out[:att_w].astype(BF16), wc=w_out[att_w:].astype(BF16), g2=row(norm2_g),
        wq_t=wq_t.astype(BF16), kbig=kbig.astype(BF16),
        u=peer_u.astype(BF16), vt=peer_v.T.astype(BF16))


def _layer(x, caches, conv_hist, w, layer):
    B, S, D = x.shape
    weights = (w["g1"], w["w_in"], w["gq"], w["gk"], w["gm"])
    if caches is None:
        q, kt, vt, kt_bf, v_bf, glu = _inproj(x, *weights, transposed=True)
        att = _attention(q, kt_bf, v_bf)
        k, v = jnp.swapaxes(kt, -1, -2), jnp.swapaxes(vt, -1, -2)
        conv_hist = jnp.zeros((B, w["cw"].shape[0] - 1, glu.shape[-1]), x.dtype)
    else:
        q, k, v, glu = _inproj(x, *weights, transposed=False)
        kt_new = jnp.swapaxes(k, -1, -2).astype(BF16)
        att = _attention(q, kt_new, v.astype(BF16), *caches, layer=layer)
    c, new_conv = _conv(glu, conv_hist, w["cw"], w["cb"], w["lg"], w["lb"])
    x1, xn = _outproj(att.reshape(B * S, -1), c.reshape(B * S, -1), x.reshape(B * S, D),
                      w["wa"], w["wc"], w["g2"])
    y = _peer(xn, x1, w["wq_t"], w["kbig"], w["u"], w["vt"])
    return y.reshape(B, S, D), k, v, new_conv


def kernel(x_prompt, x_sample, cache_k, cache_v, state_conv, norm1_g, w_in, q_norm_g, k_norm_g,
           conv_w, conv_b, conv_ln_g, conv_ln_b, w_out, norm2_g, peer_wq, peer_keys, peer_u,
           peer_v):
    y_p, y_s = x_prompt, x_sample
    cache_k = jnp.swapaxes(cache_k, -1, -2)
    cache_v = jnp.swapaxes(cache_v, -1, -2)
    outs = [[] for _ in range(6)]
    for l in range(norm1_g.shape[0]):
        w = _layer_weights(norm1_g[l], w_in[l], q_norm_g[l], k_norm_g[l], conv_w[l], conv_b[l],
                           conv_ln_g[l], conv_ln_b[l], w_out[l], norm2_g[l], peer_wq[l],
                           peer_keys[l], peer_u[l], peer_v[l])
        y_p, kp, vp, cp = _layer(y_p, None, None, w, l)
        y_s, ks, vs, cs = _layer(y_s, (cache_k, cache_v), state_conv[l], w, l)
        for lst, val in zip(outs, (kp, vp, cp, ks, vs, cs)):
            lst.append(val)
    return (y_p, y_s) + tuple(jnp.stack(o) if len(o) > 1 else o[0][None] for o in outs)
```

```python
import functools
import math

import jax
import jax.numpy as jnp
from jax import lax
from jax.experimental import pallas as pl
from jax.experimental.pallas import tpu as pltpu

F32 = jnp.float32
BF16 = jnp.bfloat16

EPS = 1e-6
HEAD_DIM = 64
PEER_HEADS = 8
PEER_TOPK = 16
LANES = 128
SUBLANES = 8
ROW_TILE = 512
CONV_ROWS = 64
ATT_BLOCK = 128
PEER_TOKENS = 512
PEER_EXPERT_CHUNK = 1024
PEER_SUB = 16
VMEM_LIMIT = 60 * 1024 * 1024
DEAD_LOG = -120.0

_NT = (((1,), (1,)), ((), ()))


def _const_spec(shape):
    zeros = (0,) * len(shape)
    return pl.BlockSpec(shape, lambda *_: zeros)


def _params(semantics):
    return pltpu.CompilerParams(dimension_semantics=semantics, vmem_limit_bytes=VMEM_LIMIT)


def _inproj_kernel(x_ref, g1_ref, w_ref, gq_ref, gk_ref, gm_ref, q_ref, *out_refs,
                   heads, att_w, conv_ch, transposed):
    x = x_ref[0]
    ms = jnp.mean(x * x, axis=-1, keepdims=True)
    hb = ((x * lax.rsqrt(ms + EPS)) * g1_ref[...]).astype(BF16)

    def proj(c0, width):
        return jnp.dot(hb, w_ref[:, c0:c0 + width], preferred_element_type=F32)

    def head_norm(t, g):
        tt = t * t
        hi = tt.astype(BF16)
        lo = (tt - hi.astype(F32)).astype(BF16)
        msq = (jnp.dot(hi, gm_ref[...], preferred_element_type=F32)
               + jnp.dot(lo, gm_ref[...], preferred_element_type=F32))
        return (t * lax.rsqrt(msq + EPS)) * g

    qn = head_norm(proj(0, att_w), gq_ref[...]) * (HEAD_DIM ** -0.5)
    kn = head_norm(proj(att_w, att_w), gk_ref[...])
    v = proj(2 * att_w, att_w)
    if transposed:
        kt_ref, vt_ref, ktb_ref, vb_ref, glu_ref = out_refs
        knt, vt = kn.T, v.T
    else:
        k_ref, v_ref, glu_ref = out_refs
    for h in range(heads):
        sl = slice(h * HEAD_DIM, (h + 1) * HEAD_DIM)
        q_ref[0, h] = qn[:, sl].astype(BF16)
        if transposed:
            kt_ref[0, h] = knt[sl, :]
            vt_ref[0, h] = vt[sl, :]
            ktb_ref[0, h] = knt[sl, :].astype(BF16)
            vb_ref[0, h] = v[:, sl].astype(BF16)
        else:
            k_ref[0, h] = kn[:, sl]
            v_ref[0, h] = v[:, sl]
    a = proj(3 * att_w, conv_ch)
    gate = proj(3 * att_w + conv_ch, conv_ch)
    glu_ref[0] = a * jax.nn.sigmoid(gate)


def _inproj(x, g1, w_bf, gq, gk, gm, transposed):
    B, S, D = x.shape
    att_w = gq.shape[-1]
    heads = att_w // HEAD_DIM
    conv_ch = (w_bf.shape[1] - 3 * att_w) // 2
    T = min(S, ROW_TILE)
    assert S % T == 0
    hm = lambda dt: jax.ShapeDtypeStruct((B, heads, S, HEAD_DIM), dt)
    hspec = pl.BlockSpec((1, heads, T, HEAD_DIM), lambda b, s: (b, 0, s, 0))
    tm = lambda dt: jax.ShapeDtypeStruct((B, heads, HEAD_DIM, S), dt)
    tspec = pl.BlockSpec((1, heads, HEAD_DIM, T), lambda b, s: (b, 0, 0, s))
    if transposed:
        kv_specs, kv_shapes = [tspec, tspec, tspec, hspec], [tm(F32), tm(F32), tm(BF16), hm(BF16)]
    else:
        kv_specs, kv_shapes = [hspec, hspec], [hm(F32), hm(F32)]
    return pl.pallas_call(
        functools.partial(_inproj_kernel, heads=heads, att_w=att_w, conv_ch=conv_ch,
                          transposed=transposed),
        grid=(B, S // T),
        in_specs=[pl.BlockSpec((1, T, D), lambda b, s: (b, s, 0)),
                  _const_spec(g1.shape), _const_spec(w_bf.shape),
                  _const_spec(gq.shape), _const_spec(gk.shape), _const_spec(gm.shape)],
        out_specs=[hspec] + kv_specs + [pl.BlockSpec((1, T, conv_ch), lambda b, s: (b, s, 0))],
        out_shape=[hm(BF16)] + kv_shapes + [jax.ShapeDtypeStruct((B, S, conv_ch), F32)],
        compiler_params=_params(("parallel", "parallel")),
        name="inproj",
    )(x, g1, w_bf, gq, gk, gm)


def _conv_kernel(glu_ref, hist_ref, w_ref, b_ref, lg_ref, lb_ref, c_ref, new_ref, ext_ref,
                 *, rows, width, chunk):
    hist = width - 1
    base = 32
    off = base - hist
    s = pl.program_id(1)

    @pl.when(s == 0)
    def _():
        ext_ref[off:base, :] = hist_ref[0]
        ext_ref[base + rows:, :] = jnp.zeros((SUBLANES, ext_ref.shape[1]), F32)

    ext_ref[base:base + rows, :] = glu_ref[0]
    span = chunk + SUBLANES
    lanes_all = ext_ref.shape[1]
    for r0 in range(0, rows, chunk):
        cols = []
        for l0 in range(0, lanes_all, LANES):
            acc = None
            for b in range(SUBLANES):
                part = None
                for a in range((off + width - 1) // SUBLANES + 1):
                    w = SUBLANES * a + b - off
                    if 0 <= w < width:
                        term = (ext_ref[r0 + SUBLANES * a:r0 + SUBLANES * a + span, l0:l0 + LANES]
                                * w_ref[w:w + 1, l0:l0 + LANES])
                        part = term if part is None else part + term
                if part is not None:
                    acc = part[b:b + chunk] if acc is None else acc + part[b:b + chunk]
            cols.append(acc)
        acc = jnp.concatenate(cols, axis=1) + b_ref[...]
        mu = jnp.mean(acc, axis=-1, keepdims=True)
        d = acc - mu
        var = jnp.mean(d * d, axis=-1, keepdims=True)
        y = d * lax.rsqrt(var + EPS) * lg_ref[...] + lb_ref[...]
        c_ref[0, r0:r0 + chunk, :] = (y * jax.nn.sigmoid(y)).astype(BF16)
    tail = ext_ref[rows + off:rows + base, :]
    new_ref[0] = tail
    ext_ref[off:base, :] = tail


def _conv(glu, hist, cw, cb, lg, lb):
    B, S, C = glu.shape
    width = cw.shape[0]
    T = min(S, ROW_TILE)
    assert S % T == 0 and (T >= width - 1 or S == T)
    chunk = min(T, CONV_ROWS)
    return pl.pallas_call(
        functools.partial(_conv_kernel, rows=T, width=width, chunk=chunk),
        grid=(B, S // T),
        in_specs=[pl.BlockSpec((1, T, C), lambda b, s: (b, s, 0)),
                  pl.BlockSpec((1, width - 1, C), lambda b, s: (b, 0, 0)),
                  _const_spec(cw.shape), _const_spec(cb.shape),
                  _const_spec(lg.shape), _const_spec(lb.shape)],
        out_specs=[pl.BlockSpec((1, T, C), lambda b, s: (b, s, 0)),
                   pl.BlockSpec((1, width - 1, C), lambda b, s: (b, 0, 0))],
        out_shape=[jax.ShapeDtypeStruct((B, S, C), BF16),
                   jax.ShapeDtypeStruct((B, width - 1, C), F32)],
        scratch_shapes=[pltpu.VMEM((32 + T + SUBLANES, C), F32)],
        compiler_params=_params(("parallel", "arbitrary")),
        name="conv",
    )(glu, hist, cw, cb, lg, lb)


def _sb_block(qs, kts, vbs, carries, upper, ones, mask, v_transposed):
    tq, tk = qs[0].shape[0], kts[0].shape[1]
    zs = [jnp.dot(q, kt, preferred_element_type=F32) for q, kt in zip(qs, kts)]
    log_betas, sums = [], []
    for z in zs:
        sp = jnp.log1p(jnp.exp(-jnp.abs(z)))
        log_beta = jnp.minimum(z, 0.0) - sp
        log_1m = log_beta - z
        if mask is not None:
            log_1m = jnp.where(mask, log_1m, 0.0)
        hi = log_1m.astype(BF16)
        lo = (log_1m - hi.astype(F32)).astype(BF16)
        both = jnp.concatenate([hi, lo], axis=0)
        if tk == LANES:
            p = jnp.dot(both, jnp.concatenate([upper, ones], axis=1),
                        preferred_element_type=F32)
            after, total = p[:, :tk], p[:, tk:]
        else:
            after = jnp.dot(both, upper, preferred_element_type=F32)
            total = jnp.dot(both, ones, preferred_element_type=F32)
        log_betas.append(log_beta)
        sums.append((after[:tq] + after[tq:], total[:tq] + total[tq:]))
    outs, new_carries = [], []
    for log_beta, (after, total), carry, vb in zip(log_betas, sums, carries, vbs):
        logw = log_beta + after
        if carry is not None:
            logw = logw + (carry if tk == LANES else carry[:, :1])
        w = jnp.exp(logw)
        if mask is not None:
            w = jnp.where(mask, w, 0.0)
        if v_transposed:
            outs.append(lax.dot_general(w.astype(BF16), vb, _NT, preferred_element_type=F32))
        else:
            outs.append(jnp.dot(w.astype(BF16), vb, preferred_element_type=F32))
        new_carries.append(total if carry is None else carry + total)
    return outs, new_carries


def _max_all(xs):
    m = xs[0]
    for x in xs[1:]:
        m = jnp.maximum(m, x)
    return jnp.max(m)


def _attn_kernel(*refs, heads, tq, tk, old_blocks, diag_in_old):
    if diag_in_old:
        q_ref, ko_ref, vo_ref, o_ref, acc_ref, carry_ref = refs
    else:
        q_ref, kd_ref, vd_ref, ko_ref, vo_ref, o_ref, acc_ref, carry_ref = refs
    qi = pl.program_id(1)

    def tri(n):
        r = lax.broadcasted_iota(jnp.int32, (n, n), 0)
        c = lax.broadcasted_iota(jnp.int32, (n, n), 1)
        return r, c

    r, c = tri(tq)
    mask = c < r
    upper_d = jnp.where(r > c, 1.0, 0.0).astype(BF16)
    ones_d = jnp.ones((tq, LANES), BF16)
    hs = range(heads)
    qs = [q_ref[0, h] for h in hs]
    if diag_in_old:
        rows = pl.ds(pl.multiple_of(qi * tq, tq), tq)
        kts = [ko_ref[0, h, :, rows] for h in hs]
        vbs = [vo_ref[0, h, rows, :] for h in hs]
    else:
        kts = [kd_ref[0, h] for h in hs]
        vbs = [vd_ref[0, h] for h in hs]
    outs, carries = _sb_block(qs, kts, vbs, [None] * heads, upper_d, ones_d, mask, False)
    for h in hs:
        acc_ref[h] = outs[h]
        carry_ref[h] = carries[h]

    if old_blocks is None:
        j0 = qi - 1
    else:
        j0 = jnp.int32(old_blocks - 1)

    r, c = tri(tk)
    upper_o = jnp.where(r > c, 1.0, 0.0).astype(BF16)
    ones_o = jnp.ones((tk, LANES), BF16)

    def cond(state):
        j, live = state
        return jnp.logical_and(j >= 0, live > DEAD_LOG)

    def body(state):
        j, _ = state
        rows = pl.ds(pl.multiple_of(j * tk, tk), tk)
        qs = [q_ref[0, h] for h in hs]
        kts = [ko_ref[0, h, :, rows].astype(BF16) for h in hs]
        if diag_in_old:
            vbs = [vo_ref[0, h, rows, :] for h in hs]
        else:
            vbs = [vo_ref[0, h, :, rows].astype(BF16) for h in hs]
        old = [carry_ref[h] for h in hs]
        accs = [acc_ref[h] for h in hs]
        outs, new = _sb_block(qs, kts, vbs, old, upper_o, ones_o, None, not diag_in_old)
        for h in hs:
            acc_ref[h] = accs[h] + outs[h]
            carry_ref[h] = new[h]
        return j - 1, _max_all(new)

    lax.while_loop(cond, body, (j0, _max_all(carries)))
    for h in range(heads):
        o_ref[0, :, h * HEAD_DIM:(h + 1) * HEAD_DIM] = acc_ref[h].astype(BF16)


def _attention(q, kt_new, v_new, kt_old=None, vt_old=None, layer=0):
    B, H, S, d = q.shape
    if kt_old is None:
        tq = tk = min(S, ATT_BLOCK)
        assert S % tq == 0
        old_blocks, diag_in_old = None, True
        in_specs = [pl.BlockSpec((1, H, tq, d), lambda b, i: (b, 0, i, 0)),
                    pl.BlockSpec((1, H, d, S), lambda b, i: (b, 0, 0, 0)),
                    pl.BlockSpec((1, H, S, d), lambda b, i: (b, 0, 0, 0))]
        args = (q, kt_new, v_new)
    else:
        P = kt_old.shape[4]
        tq, tk = S, min(P, ATT_BLOCK)
        assert P % tk == 0
        old_blocks, diag_in_old = P // tk, False
        new = pl.BlockSpec((1, H, S, d), lambda b, i: (b, 0, 0, 0))
        new_t = pl.BlockSpec((1, H, d, S), lambda b, i: (b, 0, 0, 0))
        old = pl.BlockSpec((None, 1, H, d, P), lambda b, i: (layer, b, 0, 0, 0))
        in_specs = [new, new_t, new, old, old]
        args = (q, kt_new, v_new, kt_old, vt_old)
    return pl.pallas_call(
        functools.partial(_attn_kernel, heads=H, tq=tq, tk=tk, old_blocks=old_blocks,
                          diag_in_old=diag_in_old),
        grid=(B, S // tq),
        in_specs=in_specs,
        out_specs=pl.BlockSpec((1, tq, H * d), lambda b, i: (b, i, 0)),
        out_shape=jax.ShapeDtypeStruct((B, S, H * d), BF16),
        scratch_shapes=[pltpu.VMEM((H, tq, d), F32), pltpu.VMEM((H, tq, LANES), F32)],
        compiler_params=_params(("parallel", "arbitrary")),
        name="attention",
    )(*args)


def _outproj_kernel(att_ref, c_ref, x_ref, wa_ref, wc_ref, g2_ref, x1_ref, xn_ref):
    x1 = (x_ref[...]
          + jnp.dot(att_ref[...], wa_ref[...], preferred_element_type=F32)
          + jnp.dot(c_ref[...], wc_ref[...], preferred_element_type=F32))
    x1_ref[...] = x1
    ms = jnp.mean(x1 * x1, axis=-1, keepdims=True)
    xn_ref[...] = ((x1 * lax.rsqrt(ms + EPS)) * g2_ref[...]).astype(BF16)


def _outproj(att, c, x, wa, wc, g2):
    N, D = x.shape
    T = min(N, ROW_TILE)
    assert N % T == 0
    row = lambda w: pl.BlockSpec((T, w), lambda i: (i, 0))
    return pl.pallas_call(
        _outproj_kernel,
        grid=(N // T,),
        in_specs=[row(att.shape[1]), row(c.shape[1]), row(D),
                  _const_spec(wa.shape), _const_spec(wc.shape), _const_spec(g2.shape)],
        out_specs=[row(D), row(D)],
        out_shape=[jax.ShapeDtypeStruct((N, D), F32), jax.ShapeDtypeStruct((N, D), BF16)],
        compiler_params=_params(("parallel",)),
        name="outproj",
    )(att, c, x, wa, wc, g2)


def _candidate_pairs():
    return [(p, q) for p in range(1, PEER_TOPK + 1) for q in range(1, PEER_TOPK + 1)
            if p * q <= PEER_TOPK]


def _sort_desc(v):
    v = list(v)
    n = len(v)
    k = 2
    while k <= n:
        j = k // 2
        while j >= 1:
            for i in range(n):
                m = i ^ j
                if m > i:
                    hi, lo = jnp.maximum(v[i], v[m]), jnp.minimum(v[i], v[m])
                    v[i], v[m] = (hi, lo) if (i & k) == 0 else (lo, hi)
            j //= 2
        k *= 2
    return v


def _merge_top(a, b):
    n = len(a)
    c = [jnp.maximum(a[i], b[n - 1 - i]) for i in range(n)]
    j = n // 2
    while j >= 1:
        for i in range(n):
            m = i ^ j
            if m > i:
                c[i], c[m] = jnp.maximum(c[i], c[m]), jnp.minimum(c[i], c[m])
        j //= 2
    return c


def _top_values(load, count):
    k = PEER_TOPK
    groups = []
    for g in range(0, count, k):
        vals = [load(n) for n in range(g, min(g + k, count))]
        vals += [jnp.full_like(vals[0], -jnp.inf)] * (k - len(vals))
        groups.append(_sort_desc(vals))
    while len(groups) > 1:
        merged = [_merge_top(groups[i], groups[i + 1]) for i in range(0, len(groups) - 1, 2)]
        groups = merged + ([groups[-1]] if len(groups) % 2 else [])
    return groups[0]


def _peer_route(xn_ref, wq_ref, kb_ref, xnt_ref, s_ref, top_ref, cand_ref, kth_ref,
                r0_ref, e0_ref, cnt_ref, e1_ref, *, nk, ph, tokens):
    xnt_ref[...] = xn_ref[...].astype(F32).T.astype(BF16)
    xnt = xnt_ref[...]
    for c in range(2):
        qt = jnp.dot(wq_ref[c], xnt, preferred_element_type=F32)
        sc = jnp.dot(kb_ref[c], qt.astype(BF16), preferred_element_type=F32)
        for lt in range(tokens // LANES):
            s_ref[c, lt] = sc[:, lt * LANES:(lt + 1) * LANES]

    pairs = _candidate_pairs()
    for lt in range(tokens // LANES):
        for c in range(2):
            top = _top_values(lambda n: s_ref[c, lt, n * ph:(n + 1) * ph, :], nk)
            for p in range(PEER_TOPK):
                top_ref[c, p, lt] = top[p]
        for n, (p, q) in enumerate(pairs):
            cand_ref[n, lt] = top_ref[0, p - 1, lt] + top_ref[1, q - 1, lt]
        kth_ref[lt] = _top_values(lambda n: cand_ref[n, lt], len(pairs))[PEER_TOPK - 1]
    theta = kth_ref[...]
    a1 = top_ref[0, 0]
    b1 = top_ref[1, 0]
    best = a1 + b1
    denom = None
    for n in range(len(pairs)):
        cv = cand_ref[n]
        t = jnp.where(cv >= theta, jnp.exp(cv - best), 0.0)
        denom = t if denom is None else denom + t
    inv = 1.0 / denom

    def first_half(i, _):
        rows = pl.ds(pl.multiple_of(i * ph, ph), ph)
        s0 = s_ref[0, :, rows, :]
        rank = jnp.ones_like(s0)
        for p in range(PEER_TOPK):
            rank = jnp.where(top_ref[0, p] > s0, p + 2.0, rank)
        r0_ref[:, rows, :] = _bf16_pair(rank)
        e0_ref[:, rows, :] = _bf16_pair(jnp.exp(s0 - a1) * inv)
        return 0
    lax.fori_loop(0, nk, first_half, 0)

    def second_half(j, _):
        rows = pl.ds(pl.multiple_of(j * ph, ph), ph)
        s1 = s_ref[1, :, rows, :]
        cnt = jnp.zeros_like(s1)
        for p in range(PEER_TOPK):
            cnt = jnp.where(top_ref[0, p] + s1 >= theta, p + 1.0, cnt)
        s_ref[1, :, rows, :] = cnt
        s_ref[0, :, rows, :] = jnp.exp(s1 - b1)
        return 0
    lax.fori_loop(0, nk, second_half, 0)
    for lt in range(tokens // LANES):
        for h in range(ph):
            cnt_ref[lt, h * nk:(h + 1) * nk, :] = (
                s_ref[1, lt, pl.ds(h, nk, stride=ph), :].astype(BF16))
            e1_ref[lt, h * nk:(h + 1) * nk, :] = (
                s_ref[0, lt, pl.ds(h, nk, stride=ph), :].astype(BF16))


def _bf16_pair(x):
    bits = pltpu.bitcast(x.astype(BF16).astype(F32), jnp.uint32)
    return bits | lax.shift_right_logical(bits, jnp.uint32(16))


def _zero_bits_after(x):
    bits = pltpu.bitcast(x, jnp.uint32)
    return lax.shift_right_logical(lax.shift_right_logical(bits, jnp.uint32(16)), jnp.uint32(16))


def _peer_gate_chunk(c, xu, coef_ref, r0_ref, e0_ref, cnt_ref, e1_ref, *, nk, ph, chunk, tokens):
    half_tile = (PEER_SUB // 2, LANES)

    for ib in range(chunk // nk):
        i = c * (chunk // nk) + ib
        hrows = pl.ds(pl.multiple_of(i * ph, ph), ph)
        row0 = ib * nk
        for lt in range(tokens // LANES):
            lanes = slice(lt * LANES, (lt + 1) * LANES)
            zero = _zero_bits_after(xu[row0:row0 + SUBLANES, lanes])
            rank = r0_ref[lt, hrows, :] | zero
            e0 = e0_ref[lt, hrows, :] | zero
            w = [None] * (nk // PEER_SUB)
            for h in range(ph):
                if h:
                    zero = _zero_bits_after(pltpu.bitcast(w[-1], jnp.uint32))
                rank_b = pltpu.bitcast(jnp.broadcast_to(rank[h:h + 1], half_tile) | zero, BF16)
                e0_b = pltpu.bitcast(jnp.broadcast_to(e0[h:h + 1], half_tile) | zero, BF16)
                for jb in range(nk // PEER_SUB):
                    jrows = slice(h * nk + jb * PEER_SUB, h * nk + (jb + 1) * PEER_SUB)
                    t = jnp.where(rank_b <= cnt_ref[lt, jrows, :],
                                  e1_ref[lt, jrows, :], 0.0) * e0_b
                    w[jb] = t if w[jb] is None else w[jb] + t
            for jb in range(nk // PEER_SUB):
                erows = slice(row0 + jb * PEER_SUB, row0 + (jb + 1) * PEER_SUB)
                xs = xu[erows, lanes].astype(BF16)
                act = xs * (0.5 + 0.5 * lax.erf(xs * math.sqrt(0.5)))
                coef_ref[erows, lanes] = w[jb] * act


def _peer_kernel(xn_ref, x1_ref, wq_ref, kb_ref, u_ref, vt_ref, y_ref,
                 s_ref, top_ref, cand_ref, kth_ref, r0_ref, e0_ref, cnt_ref, e1_ref,
                 acc_ref, coef_ref, xnt_ref, *, nk, ph, chunk, tokens):
    e = pl.program_id(1)

    @pl.when(e == 0)
    def _():
        _peer_route(xn_ref, wq_ref, kb_ref, xnt_ref, s_ref, top_ref, cand_ref, kth_ref,
                    r0_ref, e0_ref, cnt_ref, e1_ref, nk=nk, ph=ph, tokens=tokens)
        acc_ref[...] = jnp.zeros_like(acc_ref)

    xu = jnp.dot(u_ref[...], xnt_ref[...], preferred_element_type=F32)
    _peer_gate_chunk(e, xu, coef_ref, r0_ref, e0_ref, cnt_ref, e1_ref,
                     nk=nk, ph=ph, chunk=chunk, tokens=tokens)
    acc_ref[...] += jnp.dot(vt_ref[...], coef_ref[...], preferred_element_type=F32)

    @pl.when(e == pl.num_programs(1) - 1)
    def _():
        y_ref[...] = x1_ref[...] + acc_ref[...].T


def _peer(xn, x1, wq_t, kbig, u_bf, vt_bf):
    N, D = xn.shape
    E = u_bf.shape[0]
    ph = PEER_HEADS
    nk = kbig.shape[1] // ph
    T = min(N, PEER_TOKENS)
    chunk = min(E, PEER_EXPERT_CHUNK)
    assert N % T == 0 and E % chunk == 0 and chunk % nk == 0 and T % LANES == 0
    npairs = len(_candidate_pairs())
    lt = T // LANES
    once = dict(pipeline_mode=pl.Buffered(1))
    return pl.pallas_call(
        functools.partial(_peer_kernel, nk=nk, ph=ph, chunk=chunk, tokens=T),
        grid=(N // T, E // chunk),
        in_specs=[pl.BlockSpec((T, D), lambda t, e: (t, 0)),
                  pl.BlockSpec((T, D), lambda t, e: (t, 0), **once),
                  pl.BlockSpec(wq_t.shape, lambda t, e: (0, 0, 0), **once),
                  pl.BlockSpec(kbig.shape, lambda t, e: (0, 0, 0), **once),
                  pl.BlockSpec((chunk, D), lambda t, e: (e, 0)),
                  pl.BlockSpec((D, chunk), lambda t, e: (0, e))],
        out_specs=pl.BlockSpec((T, D), lambda t, e: (t, 0)),
        out_shape=jax.ShapeDtypeStruct((N, D), F32),
        scratch_shapes=[pltpu.VMEM((2, lt, nk * ph, LANES), F32),
                        pltpu.VMEM((2, PEER_TOPK, lt, ph, LANES), F32),
                        pltpu.VMEM((npairs, lt, ph, LANES), F32),
                        pltpu.VMEM((lt, ph, LANES), F32),
                        pltpu.VMEM((lt, nk * ph, LANES), jnp.uint32),
                        pltpu.VMEM((lt, nk * ph, LANES), jnp.uint32),
                        pltpu.VMEM((lt, ph * nk, LANES), BF16),
                        pltpu.VMEM((lt, ph * nk, LANES), BF16),
                        pltpu.VMEM((D, T), F32),
                        pltpu.VMEM((chunk, T), BF16),
                        pltpu.VMEM((D, T), BF16)],
        compiler_params=_params(("parallel", "arbitrary")),
        name="peer",
    )(xn, x1, wq_t, kbig, u_bf, vt_bf)


def _layer_weights(norm1_g, w_in, q_norm_g, k_norm_g, conv_w, conv_b, conv_ln_g, conv_ln_b,
                   w_out, norm2_g, peer_wq, peer_keys, peer_u, peer_v):
    att_w = (w_in.shape[1] - 2 * conv_w.shape[1]) // 3
    heads = att_w // HEAD_DIM
    row = lambda v: v.reshape(1, -1)
    head_of = jnp.arange(att_w) // HEAD_DIM
    gm = jnp.where(head_of[:, None] == head_of[None, :], 1.0 / HEAD_DIM, 0.0).astype(BF16)
    _, nk, half = peer_keys.shape
    ph = PEER_HEADS
    D = peer_wq.shape[0]
    wq_t = peer_wq.reshape(D, ph, 2, half).transpose(2, 1, 3, 0).reshape(2, ph * half, D)
    eye = jnp.eye(ph, dtype=peer_keys.dtype)
    kbig = (peer_keys[:, :, None, None, :] * eye[None, None, :, :, None]
            ).reshape(2, nk * ph, ph * half)
    return dict(
        g1=row(norm1_g), w_in=w_in.astype(BF16),
        gq=row(jnp.tile(q_norm_g, heads)), gk=row(jnp.tile(k_norm_g, heads)), gm=gm,
        cw=conv_w, cb=row(conv_b), lg=row(conv_ln_g), lb=row(conv_ln_b),
        wa=w_out[:att_w].astype(BF16), wc=w_out[att_w:].astype(BF16), g2=row(norm2_g),
        wq_t=wq_t.astype(BF16), kbig=kbig.astype(BF16),
        u=peer_u.astype(BF16), vt=peer_v.T.astype(BF16))


def _layer(x, caches, conv_hist, w, layer):
    B, S, D = x.shape
    weights = (w["g1"], w["w_in"], w["gq"], w["gk"], w["gm"])
    if caches is None:
        q, kt, vt, kt_bf, v_bf, glu = _inproj(x, *weights, transposed=True)
        att = _attention(q, kt_bf, v_bf)
        k, v = jnp.swapaxes(kt, -1, -2), jnp.swapaxes(vt, -1, -2)
        conv_hist = jnp.zeros((B, w["cw"].shape[0] - 1, glu.shape[-1]), x.dtype)
    else:
        q, k, v, glu = _inproj(x, *weights, transposed=False)
        kt_new = jnp.swapaxes(k, -1, -2).astype(BF16)
        att = _attention(q, kt_new, v.astype(BF16), *caches, layer=layer)
    c, new_conv = _conv(glu, conv_hist, w["cw"], w["cb"], w["lg"], w["lb"])
    x1, xn = _outproj(att.reshape(B * S, -1), c.reshape(B * S, -1), x.reshape(B * S, D),
                      w["wa"], w["wc"], w["g2"])
    y = _peer(xn, x1, w["wq_t"], w["kbig"], w["u"], w["vt"])
    return y.reshape(B, S, D), k, v, new_conv


def kernel(x_prompt, x_sample, cache_k, cache_v, state_conv, norm1_g, w_in, q_norm_g, k_norm_g,
           conv_w, conv_b, conv_ln_g, conv_ln_b, w_out, norm2_g, peer_wq, peer_keys, peer_u,
           peer_v):
    y_p, y_s = x_prompt, x_sample
    cache_k = jnp.swapaxes(cache_k, -1, -2)
    cache_v = jnp.swapaxes(cache_v, -1, -2)
    outs = [[] for _ in range(6)]
    for l in range(norm1_g.shape[0]):
        w = _layer_weights(norm1_g[l], w_in[l], q_norm_g[l], k_norm_g[l], conv_w[l], conv_b[l],
                           conv_ln_g[l], conv_ln_b[l], w_out[l], norm2_g[l], peer_wq[l],
                           peer_keys[l], peer_u[l], peer_v[l])
        y_p, kp, vp, cp = _layer(y_p, None, None, w, l)
        y_s, ks, vs, cs = _layer(y_s, (cache_k, cache_v), state_conv[l], w, l)
        for lst, val in zip(outs, (kp, vp, cp, ks, vs, cs)):
            lst.append(val)
    return (y_p, y_s) + tuple(jnp.stack(o) if len(o) > 1 else o[0][None] for o in outs)
```

```python
import functools
import math

import jax
import jax.numpy as jnp
from jax import lax
from jax.experimental import pallas as pl
from jax.experimental.pallas import tpu as pltpu

F32 = jnp.float32
BF16 = jnp.bfloat16

EPS = 1e-6
HEAD_DIM = 64
PEER_HEADS = 8
PEER_TOPK = 16
LANES = 128
SUBLANES = 8
ROW_TILE = 512
CONV_ROWS = 64
ATT_BLOCK = 128
PEER_TOKENS = 512
PEER_EXPERT_CHUNK = 1024
PEER_SUB = 16
PEER_MXU_ROWS = 256
VMEM_LIMIT = 60 * 1024 * 1024
DEAD_LOG = -120.0

_NT = (((1,), (1,)), ((), ()))


def _const_spec(shape):
    zeros = (0,) * len(shape)
    return pl.BlockSpec(shape, lambda *_: zeros)


def _params(semantics):
    return pltpu.CompilerParams(dimension_semantics=semantics, vmem_limit_bytes=VMEM_LIMIT)


def _inproj_kernel(x_ref, g1_ref, w_ref, gq_ref, gk_ref, gm_ref, q_ref, *out_refs,
                   heads, att_w, conv_ch, transposed):
    x = x_ref[0]
    ms = jnp.mean(x * x, axis=-1, keepdims=True)
    hb = ((x * lax.rsqrt(ms + EPS)) * g1_ref[...]).astype(BF16)

    def proj(c0, width):
        return jnp.dot(hb, w_ref[:, c0:c0 + width], preferred_element_type=F32)

    def head_norm(t, g):
        msq = jnp.dot((t * t).astype(BF16), gm_ref[...], preferred_element_type=F32)
        return (t * lax.rsqrt(msq + EPS)) * g

    qn = head_norm(proj(0, att_w), gq_ref[...]) * (HEAD_DIM ** -0.5)
    kn = head_norm(proj(att_w, att_w), gk_ref[...])
    v = proj(2 * att_w, att_w)
    if transposed:
        kt_ref, vt_ref, ktb_ref, vb_ref, glu_ref = out_refs
        knt, vt = kn.T, v.T
    else:
        k_ref, v_ref, glu_ref = out_refs
    for h in range(heads):
        sl = slice(h * HEAD_DIM, (h + 1) * HEAD_DIM)
        q_ref[0, h] = qn[:, sl].astype(BF16)
        if transposed:
            kt_ref[0, h] = knt[sl, :]
            vt_ref[0, h] = vt[sl, :]
            ktb_ref[0, h] = knt[sl, :].astype(BF16)
            vb_ref[0, h] = v[:, sl].astype(BF16)
        else:
            k_ref[0, h] = kn[:, sl]
            v_ref[0, h] = v[:, sl]
    a = proj(3 * att_w, conv_ch)
    gate = proj(3 * att_w + conv_ch, conv_ch)
    glu_ref[0] = a * jax.nn.sigmoid(gate)


def _inproj(x, g1, w_bf, gq, gk, gm, transposed):
    B, S, D = x.shape
    att_w = gq.shape[-1]
    heads = att_w // HEAD_DIM
    conv_ch = (w_bf.shape[1] - 3 * att_w) // 2
    T = min(S, ROW_TILE)
    assert S % T == 0
    hm = lambda dt: jax.ShapeDtypeStruct((B, heads, S, HEAD_DIM), dt)
    hspec = pl.BlockSpec((1, heads, T, HEAD_DIM), lambda b, s: (b, 0, s, 0))
    tm = lambda dt: jax.ShapeDtypeStruct((B, heads, HEAD_DIM, S), dt)
    tspec = pl.BlockSpec((1, heads, HEAD_DIM, T), lambda b, s: (b, 0, 0, s))
    if transposed:
        kv_specs, kv_shapes = [tspec, tspec, tspec, hspec], [tm(F32), tm(F32), tm(BF16), hm(BF16)]
    else:
        kv_specs, kv_shapes = [hspec, hspec], [hm(F32), hm(F32)]
    return pl.pallas_call(
        functools.partial(_inproj_kernel, heads=heads, att_w=att_w, conv_ch=conv_ch,
                          transposed=transposed),
        grid=(B, S // T),
        in_specs=[pl.BlockSpec((1, T, D), lambda b, s: (b, s, 0)),
                  _const_spec(g1.shape), _const_spec(w_bf.shape),
                  _const_spec(gq.shape), _const_spec(gk.shape), _const_spec(gm.shape)],
        out_specs=[hspec] + kv_specs + [pl.BlockSpec((1, T, conv_ch), lambda b, s: (b, s, 0))],
        out_shape=[hm(BF16)] + kv_shapes + [jax.ShapeDtypeStruct((B, S, conv_ch), F32)],
        compiler_params=_params(("parallel", "parallel")),
        name="inproj",
    )(x, g1, w_bf, gq, gk, gm)


def _conv_kernel(glu_ref, hist_ref, w_ref, b_ref, lg_ref, lb_ref, c_ref, new_ref, ext_ref,
                 *, rows, width, chunk):
    hist = width - 1
    base = 32
    off = base - hist
    s = pl.program_id(1)

    @pl.when(s == 0)
    def _():
        ext_ref[off:base, :] = hist_ref[0]
        ext_ref[base + rows:, :] = jnp.zeros((SUBLANES, ext_ref.shape[1]), F32)

    ext_ref[base:base + rows, :] = glu_ref[0]
    span = chunk + SUBLANES
    lanes_all = ext_ref.shape[1]
    for r0 in range(0, rows, chunk):
        cols = []
        for l0 in range(0, lanes_all, LANES):
            acc = None
            for b in range(SUBLANES):
                part = None
                for a in range((off + width - 1) // SUBLANES + 1):
                    w = SUBLANES * a + b - off
                    if 0 <= w < width:
                        term = (ext_ref[r0 + SUBLANES * a:r0 + SUBLANES * a + span, l0:l0 + LANES]
                                * w_ref[w:w + 1, l0:l0 + LANES])
                        part = term if part is None else part + term
                if part is not None:
                    acc = part[b:b + chunk] if acc is None else acc + part[b:b + chunk]
            cols.append(acc)
        acc = jnp.concatenate(cols, axis=1) + b_ref[...]
        mu = jnp.mean(acc, axis=-1, keepdims=True)
        d = acc - mu
        var = jnp.mean(d * d, axis=-1, keepdims=True)
        y = d * lax.rsqrt(var + EPS) * lg_ref[...] + lb_ref[...]
        c_ref[0, r0:r0 + chunk, :] = (y * jax.nn.sigmoid(y)).astype(BF16)
    tail = ext_ref[rows + off:rows + base, :]
    new_ref[0] = tail
    ext_ref[off:base, :] = tail


def _conv(glu, hist, cw, cb, lg, lb):
    B, S, C = glu.shape
    width = cw.shape[0]
    T = min(S, ROW_TILE)
    assert S % T == 0 and (T >= width - 1 or S == T)
    chunk = min(T, CONV_ROWS)
    return pl.pallas_call(
        functools.partial(_conv_kernel, rows=T, width=width, chunk=chunk),
        grid=(B, S // T),
        in_specs=[pl.BlockSpec((1, T, C), lambda b, s: (b, s, 0)),
                  pl.BlockSpec((1, width - 1, C), lambda b, s: (b, 0, 0)),
                  _const_spec(cw.shape), _const_spec(cb.shape),
                  _const_spec(lg.shape), _const_spec(lb.shape)],
        out_specs=[pl.BlockSpec((1, T, C), lambda b, s: (b, s, 0)),
                   pl.BlockSpec((1, width - 1, C), lambda b, s: (b, 0, 0))],
        out_shape=[jax.ShapeDtypeStruct((B, S, C), BF16),
                   jax.ShapeDtypeStruct((B, width - 1, C), F32)],
        scratch_shapes=[pltpu.VMEM((32 + T + SUBLANES, C), F32)],
        compiler_params=_params(("parallel", "arbitrary")),
        name="conv",
    )(glu, hist, cw, cb, lg, lb)


def _sb_block(qs, kts, vbs, carries, upper, ones, mask, v_transposed):
    tq, tk = qs[0].shape[0], kts[0].shape[1]
    zs = [jnp.dot(q, kt, preferred_element_type=F32) for q, kt in zip(qs, kts)]
    log_betas, sums = [], []
    for z in zs:
        sp = jnp.log1p(jnp.exp(-jnp.abs(z)))
        log_beta = jnp.minimum(z, 0.0) - sp
        log_1m = log_beta - z
        if mask is not None:
            log_1m = jnp.where(mask, log_1m, 0.0)
        hi = log_1m.astype(BF16)
        lo = (log_1m - hi.astype(F32)).astype(BF16)
        both = jnp.concatenate([hi, lo], axis=0)
        if tk == LANES:
            p = jnp.dot(both, jnp.concatenate([upper, ones], axis=1),
                        preferred_element_type=F32)
            after, total = p[:, :tk], p[:, tk:]
        else:
            after = jnp.dot(both, upper, preferred_element_type=F32)
            total = jnp.dot(both, ones, preferred_element_type=F32)
        log_betas.append(log_beta)
        sums.append((after[:tq] + after[tq:], total[:tq] + total[tq:]))
    outs, new_carries = [], []
    for log_beta, (after, total), carry, vb in zip(log_betas, sums, carries, vbs):
        logw = log_beta + after
        if carry is not None:
            logw = logw + (carry if tk == LANES else carry[:, :1])
        w = jnp.exp(logw)
        if mask is not None:
            w = jnp.where(mask, w, 0.0)
        if v_transposed:
            outs.append(lax.dot_general(w.astype(BF16), vb, _NT, preferred_element_type=F32))
        else:
            outs.append(jnp.dot(w.astype(BF16), vb, preferred_element_type=F32))
        new_carries.append(total if carry is None else carry + total)
    return outs, new_carries


def _max_all(xs):
    m = xs[0]
    for x in xs[1:]:
        m = jnp.maximum(m, x)
    return jnp.max(m)


def _attn_kernel(*refs, heads, tq, tk, old_blocks, diag_in_old):
    if diag_in_old:
        q_ref, ko_ref, vo_ref, o_ref, acc_ref, carry_ref = refs
    else:
        q_ref, kd_ref, vd_ref, ko_ref, vo_ref, o_ref, acc_ref, carry_ref = refs
    qi = pl.program_id(1)

    def tri(n):
        r = lax.broadcasted_iota(jnp.int32, (n, n), 0)
        c = lax.broadcasted_iota(jnp.int32, (n, n), 1)
        return r, c

    r, c = tri(tq)
    mask = c < r
    upper_d = jnp.where(r > c, 1.0, 0.0).astype(BF16)
    ones_d = jnp.ones((tq, LANES), BF16)
    hs = range(heads)
    qs = [q_ref[0, h] for h in hs]
    if diag_in_old:
        rows = pl.ds(pl.multiple_of(qi * tq, tq), tq)
        kts = [ko_ref[0, h, :, rows] for h in hs]
        vbs = [vo_ref[0, h, rows, :] for h in hs]
    else:
        kts = [kd_ref[0, h] for h in hs]
        vbs = [vd_ref[0, h] for h in hs]
    outs, carries = _sb_block(qs, kts, vbs, [None] * heads, upper_d, ones_d, mask, False)
    for h in hs:
        acc_ref[h] = outs[h]
        carry_ref[h] = carries[h]

    if old_blocks is None:
        j0 = qi - 1
    else:
        j0 = jnp.int32(old_blocks - 1)

    r, c = tri(tk)
    upper_o = jnp.where(r > c, 1.0, 0.0).astype(BF16)
    ones_o = jnp.ones((tk, LANES), BF16)

    def cond(state):
        j, live = state
        return jnp.logical_and(j >= 0, live > DEAD_LOG)

    def body(state):
        j, _ = state
        rows = pl.ds(pl.multiple_of(j * tk, tk), tk)
        qs = [q_ref[0, h] for h in hs]
        kts = [ko_ref[0, h, :, rows].astype(BF16) for h in hs]
        if diag_in_old:
            vbs = [vo_ref[0, h, rows, :] for h in hs]
        else:
            vbs = [vo_ref[0, h, :, rows].astype(BF16) for h in hs]
        old = [carry_ref[h] for h in hs]
        accs = [acc_ref[h] for h in hs]
        outs, new = _sb_block(qs, kts, vbs, old, upper_o, ones_o, None, not diag_in_old)
        for h in hs:
            acc_ref[h] = accs[h] + outs[h]
            carry_ref[h] = new[h]
        return j - 1, _max_all(new)

    lax.while_loop(cond, body, (j0, _max_all(carries)))
    for h in range(heads):
        o_ref[0, :, h * HEAD_DIM:(h + 1) * HEAD_DIM] = acc_ref[h].astype(BF16)


def _attention(q, kt_new, v_new, kt_old=None, vt_old=None, layer=0):
    B, H, S, d = q.shape
    if kt_old is None:
        tq = tk = min(S, ATT_BLOCK)
        assert S % tq == 0
        old_blocks, diag_in_old = None, True
        in_specs = [pl.BlockSpec((1, H, tq, d), lambda b, i: (b, 0, i, 0)),
                    pl.BlockSpec((1, H, d, S), lambda b, i: (b, 0, 0, 0)),
                    pl.BlockSpec((1, H, S, d), lambda b, i: (b, 0, 0, 0))]
        args = (q, kt_new, v_new)
    else:
        P = kt_old.shape[4]
        tq, tk = S, min(P, ATT_BLOCK)
        assert P % tk == 0
        old_blocks, diag_in_old = P // tk, False
        new = pl.BlockSpec((1, H, S, d), lambda b, i: (b, 0, 0, 0))
        new_t = pl.BlockSpec((1, H, d, S), lambda b, i: (b, 0, 0, 0))
        old = pl.BlockSpec((None, 1, H, d, P), lambda b, i: (layer, b, 0, 0, 0))
        in_specs = [new, new_t, new, old, old]
        args = (q, kt_new, v_new, kt_old, vt_old)
    return pl.pallas_call(
        functools.partial(_attn_kernel, heads=H, tq=tq, tk=tk, old_blocks=old_blocks,
                          diag_in_old=diag_in_old),
        grid=(B, S // tq),
        in_specs=in_specs,
        out_specs=pl.BlockSpec((1, tq, H * d), lambda b, i: (b, i, 0)),
        out_shape=jax.ShapeDtypeStruct((B, S, H * d), BF16),
        scratch_shapes=[pltpu.VMEM((H, tq, d), F32), pltpu.VMEM((H, tq, LANES), F32)],
        compiler_params=_params(("parallel", "arbitrary")),
        name="attention",
    )(*args)


def _outproj_kernel(att_ref, c_ref, x_ref, wa_ref, wc_ref, g2_ref, x1_ref, xn_ref):
    x1 = (x_ref[...]
          + jnp.dot(att_ref[...], wa_ref[...], preferred_element_type=F32)
          + jnp.dot(c_ref[...], wc_ref[...], preferred_element_type=F32))
    x1_ref[...] = x1
    ms = jnp.mean(x1 * x1, axis=-1, keepdims=True)
    xn_ref[...] = ((x1 * lax.rsqrt(ms + EPS)) * g2_ref[...]).astype(BF16)


def _outproj(att, c, x, wa, wc, g2):
    N, D = x.shape
    T = min(N, ROW_TILE)
    assert N % T == 0
    row = lambda w: pl.BlockSpec((T, w), lambda i: (i, 0))
    return pl.pallas_call(
        _outproj_kernel,
        grid=(N // T,),
        in_specs=[row(att.shape[1]), row(c.shape[1]), row(D),
                  _const_spec(wa.shape), _const_spec(wc.shape), _const_spec(g2.shape)],
        out_specs=[row(D), row(D)],
        out_shape=[jax.ShapeDtypeStruct((N, D), F32), jax.ShapeDtypeStruct((N, D), BF16)],
        compiler_params=_params(("parallel",)),
        name="outproj",
    )(att, c, x, wa, wc, g2)


def _candidate_pairs():
    return [(p, q) for p in range(1, PEER_TOPK + 1) for q in range(1, PEER_TOPK + 1)
            if p * q <= PEER_TOPK]


def _sort_desc(v):
    v = list(v)
    n = len(v)
    k = 2
    while k <= n:
        j = k // 2
        while j >= 1:
            for i in range(n):
                m = i ^ j
                if m > i:
                    hi, lo = jnp.maximum(v[i], v[m]), jnp.minimum(v[i], v[m])
                    v[i], v[m] = (hi, lo) if (i & k) == 0 else (lo, hi)
            j //= 2
        k *= 2
    return v


def _merge_top(a, b):
    n = len(a)
    c = [jnp.maximum(a[i], b[n - 1 - i]) for i in range(n)]
    j = n // 2
    while j >= 1:
        for i in range(n):
            m = i ^ j
            if m > i:
                c[i], c[m] = jnp.maximum(c[i], c[m]), jnp.minimum(c[i], c[m])
        j //= 2
    return c


def _top_values(load, count):
    k = PEER_TOPK
    groups = []
    for g in range(0, count, k):
        vals = [load(n) for n in range(g, min(g + k, count))]
        vals += [jnp.full_like(vals[0], -jnp.inf)] * (k - len(vals))
        groups.append(_sort_desc(vals))
    while len(groups) > 1:
        merged = [_merge_top(groups[i], groups[i + 1]) for i in range(0, len(groups) - 1, 2)]
        groups = merged + ([groups[-1]] if len(groups) % 2 else [])
    return groups[0]


def _peer_route(xn_ref, wq_ref, kb_ref, xnt_ref, s_ref, top_ref, cand_ref, kth_ref,
                r0_ref, e0_ref, cnt_ref, e1_ref, *, nk, ph, tokens):
    xnt_ref[...] = xn_ref[...].astype(F32).T.astype(BF16)
    xnt = xnt_ref[...]
    for c in range(2):
        qt = jnp.dot(wq_ref[c], xnt, preferred_element_type=F32)
        sc = jnp.dot(kb_ref[c], qt.astype(BF16), preferred_element_type=F32)
        for lt in range(tokens // LANES):
            s_ref[c, lt] = sc[:, lt * LANES:(lt + 1) * LANES]

    pairs = _candidate_pairs()
    for lt in range(tokens // LANES):
        for c in range(2):
            top = _top_values(lambda n: s_ref[c, lt, n * ph:(n + 1) * ph, :], nk)
            for p in range(PEER_TOPK):
                top_ref[c, p, lt] = top[p]
        for n, (p, q) in enumerate(pairs):
            cand_ref[n, lt] = top_ref[0, p - 1, lt] + top_ref[1, q - 1, lt]
        kth_ref[lt] = _top_values(lambda n: cand_ref[n, lt], len(pairs))[PEER_TOPK - 1]
    theta = kth_ref[...]
    a1 = top_ref[0, 0]
    b1 = top_ref[1, 0]
    best = a1 + b1
    denom = None
    for n in range(len(pairs)):
        cv = cand_ref[n]
        t = jnp.where(cv >= theta, jnp.exp(cv - best), 0.0)
        denom = t if denom is None else denom + t
    inv = 1.0 / denom

    def first_half(i, _):
        rows = pl.ds(pl.multiple_of(i * ph, ph), ph)
        s0 = s_ref[0, :, rows, :]
        rank = jnp.ones_like(s0)
        for p in range(PEER_TOPK):
            rank = jnp.where(top_ref[0, p] > s0, p + 2.0, rank)
        r0_ref[:, rows, :] = _bf16_pair(rank)
        e0_ref[:, rows, :] = _bf16_pair(jnp.exp(s0 - a1) * inv)
        return 0
    lax.fori_loop(0, nk, first_half, 0)

    def second_half(j, _):
        rows = pl.ds(pl.multiple_of(j * ph, ph), ph)
        s1 = s_ref[1, :, rows, :]
        cnt = jnp.zeros_like(s1)
        for p in range(PEER_TOPK):
            cnt = jnp.where(top_ref[0, p] + s1 >= theta, p + 1.0, cnt)
        s_ref[1, :, rows, :] = cnt
        s_ref[0, :, rows, :] = jnp.exp(s1 - b1)
        return 0
    lax.fori_loop(0, nk, second_half, 0)
    for lt in range(tokens // LANES):
        for h in range(ph):
            cnt_ref[lt, h * nk:(h + 1) * nk, :] = (
                s_ref[1, lt, pl.ds(h, nk, stride=ph), :].astype(BF16))
            e1_ref[lt, h * nk:(h + 1) * nk, :] = (
                s_ref[0, lt, pl.ds(h, nk, stride=ph), :].astype(BF16))


def _bf16_pair(x):
    bits = pltpu.bitcast(x.astype(BF16).astype(F32), jnp.uint32)
    return bits | lax.shift_right_logical(bits, jnp.uint32(16))


def _zero_bits_after(x):
    bits = pltpu.bitcast(x, jnp.uint32)
    return lax.shift_right_logical(lax.shift_right_logical(bits, jnp.uint32(16)), jnp.uint32(16))


def _peer_gate_chunk(c, xu, coef_ref, r0_ref, e0_ref, cnt_ref, e1_ref, *, nk, ph, chunk, tokens):
    half_tile = (PEER_SUB // 2, LANES)

    for ib in range(chunk // nk):
        i = c * (chunk // nk) + ib
        hrows = pl.ds(pl.multiple_of(i * ph, ph), ph)
        row0 = ib * nk
        for lt in range(tokens // LANES):
            lanes = slice(lt * LANES, (lt + 1) * LANES)
            zero = _zero_bits_after(xu[row0:row0 + SUBLANES, lanes])
            rank = r0_ref[lt, hrows, :] | zero
            e0 = e0_ref[lt, hrows, :] | zero
            w = [None] * (nk // PEER_SUB)
            for h in range(ph):
                if h:
                    zero = _zero_bits_after(pltpu.bitcast(w[-1], jnp.uint32))
                rank_b = pltpu.bitcast(jnp.broadcast_to(rank[h:h + 1], half_tile) | zero, BF16)
                e0_b = pltpu.bitcast(jnp.broadcast_to(e0[h:h + 1], half_tile) | zero, BF16)
                for jb in range(nk // PEER_SUB):
                    jrows = slice(h * nk + jb * PEER_SUB, h * nk + (jb + 1) * PEER_SUB)
                    t = jnp.where(rank_b <= cnt_ref[lt, jrows, :],
                                  e1_ref[lt, jrows, :], 0.0) * e0_b
                    w[jb] = t if w[jb] is None else w[jb] + t
            for jb in range(nk // PEER_SUB):
                erows = slice(row0 + jb * PEER_SUB, row0 + (jb + 1) * PEER_SUB)
                xs = xu[erows, lanes].astype(BF16)
                act = xs * (0.5 + 0.5 * lax.erf(xs * math.sqrt(0.5)))
                coef_ref[erows, lanes] = w[jb] * act


def _peer_kernel(xn_ref, x1_ref, wq_ref, kb_ref, u_ref, vt_ref, y_ref,
                 s_ref, top_ref, cand_ref, kth_ref, r0_ref, e0_ref, cnt_ref, e1_ref,
                 acc_ref, coef_ref, xnt_ref, *, nk, ph, chunk, tokens):
    e = pl.program_id(1)

    @pl.when(e == 0)
    def _():
        _peer_route(xn_ref, wq_ref, kb_ref, xnt_ref, s_ref, top_ref, cand_ref, kth_ref,
                    r0_ref, e0_ref, cnt_ref, e1_ref, nk=nk, ph=ph, tokens=tokens)
        acc_ref[...] = jnp.zeros_like(acc_ref)

    xnt = xnt_ref[...]
    groups = chunk // PEER_MXU_ROWS
    for g in range(groups):
        rows = slice(g * PEER_MXU_ROWS, (g + 1) * PEER_MXU_ROWS)
        xu = jnp.dot(u_ref[rows, :], xnt, preferred_element_type=F32)
        _peer_gate_chunk(e * groups + g, xu, coef_ref.at[rows, :], r0_ref, e0_ref, cnt_ref, e1_ref,
                         nk=nk, ph=ph, chunk=PEER_MXU_ROWS, tokens=tokens)
    acc_ref[...] += jnp.dot(vt_ref[...], coef_ref[...], preferred_element_type=F32)

    @pl.when(e == pl.num_programs(1) - 1)
    def _():
        y_ref[...] = x1_ref[...] + acc_ref[...].T


def _peer(xn, x1, wq_t, kbig, u_bf, vt_bf):
    N, D = xn.shape
    E = u_bf.shape[0]
    ph = PEER_HEADS
    nk = kbig.shape[1] // ph
    T = min(N, PEER_TOKENS)
    chunk = min(E, PEER_EXPERT_CHUNK)
    assert N % T == 0 and E % chunk == 0 and T % LANES == 0
    assert chunk % PEER_MXU_ROWS == 0 and PEER_MXU_ROWS % nk == 0
    npairs = len(_candidate_pairs())
    lt = T // LANES
    once = dict(pipeline_mode=pl.Buffered(1))
    return pl.pallas_call(
        functools.partial(_peer_kernel, nk=nk, ph=ph, chunk=chunk, tokens=T),
        grid=(N // T, E // chunk),
        in_specs=[pl.BlockSpec((T, D), lambda t, e: (t, 0)),
                  pl.BlockSpec((T, D), lambda t, e: (t, 0), **once),
                  pl.BlockSpec(wq_t.shape, lambda t, e: (0, 0, 0), **once),
                  pl.BlockSpec(kbig.shape, lambda t, e: (0, 0, 0), **once),
                  pl.BlockSpec((chunk, D), lambda t, e: (e, 0)),
                  pl.BlockSpec((D, chunk), lambda t, e: (0, e))],
        out_specs=pl.BlockSpec((T, D), lambda t, e: (t, 0)),
        out_shape=jax.ShapeDtypeStruct((N, D), F32),
        scratch_shapes=[pltpu.VMEM((2, lt, nk * ph, LANES), F32),
                        pltpu.VMEM((2, PEER_TOPK, lt, ph, LANES), F32),
                        pltpu.VMEM((npairs, lt, ph, LANES), F32),
                        pltpu.VMEM((lt, ph, LANES), F32),
                        pltpu.VMEM((lt, nk * ph, LANES), jnp.uint32),
                        pltpu.VMEM((lt, nk * ph, LANES), jnp.uint32),
                        pltpu.VMEM((lt, ph * nk, LANES), BF16),
                        pltpu.VMEM((lt, ph * nk, LANES), BF16),
                        pltpu.VMEM((D, T), F32),
                        pltpu.VMEM((chunk, T), BF16),
                        pltpu.VMEM((D, T), BF16)],
        compiler_params=_params(("parallel", "arbitrary")),
        name="peer",
    )(xn, x1, wq_t, kbig, u_bf, vt_bf)


def _layer_weights(norm1_g, w_in, q_norm_g, k_norm_g, conv_w, conv_b, conv_ln_g, conv_ln_b,
                   w_out, norm2_g, peer_wq, peer_keys, peer_u, peer_v):
    att_w = (w_in.shape[1] - 2 * conv_w.shape[1]) // 3
    heads = att_w // HEAD_DIM
    row = lambda v: v.reshape(1, -1)
    head_of = jnp.arange(att_w) // HEAD_DIM
    gm = jnp.where(head_of[:, None] == head_of[None, :], 1.0 / HEAD_DIM, 0.0).astype(BF16)
    _, nk, half = peer_keys.shape
    ph = PEER_HEADS
    D = peer_wq.shape[0]
    wq_t = peer_wq.reshape(D, ph, 2, half).transpose(2, 1, 3, 0).reshape(2, ph * half, D)
    eye = jnp.eye(ph, dtype=peer_keys.dtype)
    kbig = (peer_keys[:, :, None, None, :] * eye[None, None, :, :, None]
            ).reshape(2, nk * ph, ph * half)
    return dict(
        g1=row(norm1_g), w_in=w_in.astype(BF16),
        gq=row(jnp.tile(q_norm_g, heads)), gk=row(jnp.tile(k_norm_g, heads)), gm=gm,
        cw=conv_w, cb=row(conv_b), lg=row(conv_ln_g), lb=row(conv_ln_b),
        wa=w_out[:att_w].astype(BF16), wc=w_out[att_w:].astype(BF16), g2=row(norm2_g),
        wq_t=wq_t.astype(BF16), kbig=kbig.astype(BF16),
        u=peer_u.astype(BF16), vt=peer_v.T.astype(BF16))


def _layer(x, caches, conv_hist, w, layer):
    B, S, D = x.shape
    weights = (w["g1"], w["w_in"], w["gq"], w["gk"], w["gm"])
    if caches is None:
        q, kt, vt, kt_bf, v_bf, glu = _inproj(x, *weights, transposed=True)
        att = _attention(q, kt_bf, v_bf)
        k, v = jnp.swapaxes(kt, -1, -2), jnp.swapaxes(vt, -1, -2)
        conv_hist = jnp.zeros((B, w["cw"].shape[0] - 1, glu.shape[-1]), x.dtype)
    else:
        q, k, v, glu = _inproj(x, *weights, transposed=False)
        kt_new = jnp.swapaxes(k, -1, -2).astype(BF16)
        att = _attention(q, kt_new, v.astype(BF16), *caches, layer=layer)
    c, new_conv = _conv(glu, conv_hist, w["cw"], w["cb"], w["lg"], w["lb"])
    x1, xn = _outproj(att.reshape(B * S, -1), c.reshape(B * S, -1), x.reshape(B * S, D),
                      w["wa"], w["wc"], w["g2"])
    y = _peer(xn, x1, w["wq_t"], w["kbig"], w["u"], w["vt"])
    return y.reshape(B, S, D), k, v, new_conv


def kernel(x_prompt, x_sample, cache_k, cache_v, state_conv, norm1_g, w_in, q_norm_g, k_norm_g,
           conv_w, conv_b, conv_ln_g, conv_ln_b, w_out, norm2_g, peer_wq, peer_keys, peer_u,
           peer_v):
    y_p, y_s = x_prompt, x_sample
    cache_k = jnp.swapaxes(cache_k, -1, -2)
    cache_v = jnp.swapaxes(cache_v, -1, -2)
    outs = [[] for _ in range(6)]
    for l in range(norm1_g.shape[0]):
        w = _layer_weights(norm1_g[l], w_in[l], q_norm_g[l], k_norm_g[l], conv_w[l], conv_b[l],
                           conv_ln_g[l], conv_ln_b[l], w_out[l], norm2_g[l], peer_wq[l],
                           peer_keys[l], peer_u[l], peer_v[l])
        y_p, kp, vp, cp = _layer(y_p, None, None, w, l)
        y_s, ks, vs, cs = _layer(y_s, (cache_k, cache_v), state_conv[l], w, l)
        for lst, val in zip(outs, (kp, vp, cp, ks, vs, cs)):
            lst.append(val)
    return (y_p, y_s) + tuple(jnp.stack(o) if len(o) > 1 else o[0][None] for o in outs)
```

```python
import functools
import math

import jax
import jax.numpy as jnp
from jax import lax
from jax.experimental import pallas as pl
from jax.experimental.pallas import tpu as pltpu

F32 = jnp.float32
BF16 = jnp.bfloat16

EPS = 1e-6
HEAD_DIM = 64
PEER_HEADS = 8
PEER_TOPK = 16
LANES = 128
SUBLANES = 8
ROW_TILE = 512
CONV_ROWS = 64
ATT_BLOCK = 128
PEER_TOKENS = 512
PEER_EXPERT_CHUNK = 1024
PEER_SUB = 16
VMEM_LIMIT = 60 * 1024 * 1024
DEAD_LOG = -120.0

_NT = (((1,), (1,)), ((), ()))


def _const_spec(shape):
    zeros = (0,) * len(shape)
    return pl.BlockSpec(shape, lambda *_: zeros)


def _params(semantics):
    return pltpu.CompilerParams(dimension_semantics=semantics, vmem_limit_bytes=VMEM_LIMIT)


def _inproj_kernel(x_ref, g1_ref, w_ref, gq_ref, gk_ref, gm_ref, q_ref, *out_refs,
                   heads, att_w, conv_ch, transposed):
    x = x_ref[0]
    ms = jnp.mean(x * x, axis=-1, keepdims=True)
    hb = ((x * lax.rsqrt(ms + EPS)) * g1_ref[...]).astype(BF16)

    def proj(c0, width):
        return jnp.dot(hb, w_ref[:, c0:c0 + width], preferred_element_type=F32)

    def head_norm(t, g):
        msq = jnp.dot((t * t).astype(BF16), gm_ref[...], preferred_element_type=F32)
        return (t * lax.rsqrt(msq + EPS)) * g

    qn = head_norm(proj(0, att_w), gq_ref[...]) * (HEAD_DIM ** -0.5)
    kn = head_norm(proj(att_w, att_w), gk_ref[...])
    v = proj(2 * att_w, att_w)
    if transposed:
        kt_ref, vt_ref, ktb_ref, vb_ref, glu_ref = out_refs
        knt, vt = kn.T, v.T
    else:
        k_ref, v_ref, glu_ref = out_refs
    for h in range(heads):
        sl = slice(h * HEAD_DIM, (h + 1) * HEAD_DIM)
        q_ref[0, h] = qn[:, sl].astype(BF16)
        if transposed:
            kt_ref[0, h] = knt[sl, :]
            vt_ref[0, h] = vt[sl, :]
            ktb_ref[0, h] = knt[sl, :].astype(BF16)
            vb_ref[0, h] = v[:, sl].astype(BF16)
        else:
            k_ref[0, h] = kn[:, sl]
            v_ref[0, h] = v[:, sl]
    a = proj(3 * att_w, conv_ch)
    gate = proj(3 * att_w + conv_ch, conv_ch)
    glu_ref[0] = a * jax.nn.sigmoid(gate)


def _inproj(x, g1, w_bf, gq, gk, gm, transposed):
    B, S, D = x.shape
    att_w = gq.shape[-1]
    heads = att_w // HEAD_DIM
    conv_ch = (w_bf.shape[1] - 3 * att_w) // 2
    T = min(S, ROW_TILE)
    assert S % T == 0
    hm = lambda dt: jax.ShapeDtypeStruct((B, heads, S, HEAD_DIM), dt)
    hspec = pl.BlockSpec((1, heads, T, HEAD_DIM), lambda b, s: (b, 0, s, 0))
    tm = lambda dt: jax.ShapeDtypeStruct((B, heads, HEAD_DIM, S), dt)
    tspec = pl.BlockSpec((1, heads, HEAD_DIM, T), lambda b, s: (b, 0, 0, s))
    if transposed:
        kv_specs, kv_shapes = [tspec, tspec, tspec, hspec], [tm(F32), tm(F32), tm(BF16), hm(BF16)]
    else:
        kv_specs, kv_shapes = [hspec, hspec], [hm(F32), hm(F32)]
    return pl.pallas_call(
        functools.partial(_inproj_kernel, heads=heads, att_w=att_w, conv_ch=conv_ch,
                          transposed=transposed),
        grid=(B, S // T),
        in_specs=[pl.BlockSpec((1, T, D), lambda b, s: (b, s, 0)),
                  _const_spec(g1.shape), _const_spec(w_bf.shape),
                  _const_spec(gq.shape), _const_spec(gk.shape), _const_spec(gm.shape)],
        out_specs=[hspec] + kv_specs + [pl.BlockSpec((1, T, conv_ch), lambda b, s: (b, s, 0))],
        out_shape=[hm(BF16)] + kv_shapes + [jax.ShapeDtypeStruct((B, S, conv_ch), F32)],
        compiler_params=_params(("parallel", "parallel")),
        name="inproj",
    )(x, g1, w_bf, gq, gk, gm)


def _conv_kernel(glu_ref, hist_ref, w_ref, b_ref, lg_ref, lb_ref, c_ref, new_ref, ext_ref,
                 *, rows, width, chunk):
    hist = width - 1
    base = 32
    off = base - hist
    s = pl.program_id(1)

    @pl.when(s == 0)
    def _():
        ext_ref[off:base, :] = hist_ref[0]
        ext_ref[base + rows:, :] = jnp.zeros((SUBLANES, ext_ref.shape[1]), F32)

    ext_ref[base:base + rows, :] = glu_ref[0]
    span = chunk + SUBLANES
    lanes_all = ext_ref.shape[1]
    for r0 in range(0, rows, chunk):
        cols = []
        for l0 in range(0, lanes_all, LANES):
            acc = None
            for b in range(SUBLANES):
                part = None
                for a in range((off + width - 1) // SUBLANES + 1):
                    w = SUBLANES * a + b - off
                    if 0 <= w < width:
                        term = (ext_ref[r0 + SUBLANES * a:r0 + SUBLANES * a + span, l0:l0 + LANES]
                                * w_ref[w:w + 1, l0:l0 + LANES])
                        part = term if part is None else part + term
                if part is not None:
                    acc = part[b:b + chunk] if acc is None else acc + part[b:b + chunk]
            cols.append(acc)
        acc = jnp.concatenate(cols, axis=1) + b_ref[...]
        mu = jnp.mean(acc, axis=-1, keepdims=True)
        d = acc - mu
        var = jnp.mean(d * d, axis=-1, keepdims=True)
        y = d * lax.rsqrt(var + EPS) * lg_ref[...] + lb_ref[...]
        c_ref[0, r0:r0 + chunk, :] = (y * jax.nn.sigmoid(y)).astype(BF16)
    tail = ext_ref[rows + off:rows + base, :]
    new_ref[0] = tail
    ext_ref[off:base, :] = tail


def _conv(glu, hist, cw, cb, lg, lb):
    B, S, C = glu.shape
    width = cw.shape[0]
    T = min(S, ROW_TILE)
    assert S % T == 0 and (T >= width - 1 or S == T)
    chunk = min(T, CONV_ROWS)
    return pl.pallas_call(
        functools.partial(_conv_kernel, rows=T, width=width, chunk=chunk),
        grid=(B, S // T),
        in_specs=[pl.BlockSpec((1, T, C), lambda b, s: (b, s, 0)),
                  pl.BlockSpec((1, width - 1, C), lambda b, s: (b, 0, 0)),
                  _const_spec(cw.shape), _const_spec(cb.shape),
                  _const_spec(lg.shape), _const_spec(lb.shape)],
        out_specs=[pl.BlockSpec((1, T, C), lambda b, s: (b, s, 0)),
                   pl.BlockSpec((1, width - 1, C), lambda b, s: (b, 0, 0))],
        out_shape=[jax.ShapeDtypeStruct((B, S, C), BF16),
                   jax.ShapeDtypeStruct((B, width - 1, C), F32)],
        scratch_shapes=[pltpu.VMEM((32 + T + SUBLANES, C), F32)],
        compiler_params=_params(("parallel", "arbitrary")),
        name="conv",
    )(glu, hist, cw, cb, lg, lb)


def _sb_block(qs, kts, vbs, carries, upper, ones, mask, v_transposed):
    tq, tk = qs[0].shape[0], kts[0].shape[1]
    zs = [jnp.dot(q, kt, preferred_element_type=F32) for q, kt in zip(qs, kts)]
    log_betas, terms = [], []
    for z in zs:
        sp = jnp.log1p(jnp.exp(-jnp.abs(z)))
        log_beta = jnp.minimum(z, 0.0) - sp
        log_1m = log_beta - z
        if mask is not None:
            log_1m = jnp.where(mask, log_1m, 0.0)
        hi = log_1m.astype(BF16)
        lo = (log_1m - hi.astype(F32)).astype(BF16)
        log_betas.append(log_beta)
        terms += [hi, lo]
    both = jnp.concatenate(terms, axis=0)
    if tk == LANES:
        p = jnp.dot(both, jnp.concatenate([upper, ones], axis=1), preferred_element_type=F32)
        after, total = p[:, :tk], p[:, tk:]
    else:
        after = jnp.dot(both, upper, preferred_element_type=F32)
        total = jnp.dot(both, ones, preferred_element_type=F32)
    sums = []
    for h in range(len(zs)):
        r = 2 * h * tq
        sums.append((after[r:r + tq] + after[r + tq:r + 2 * tq],
                     total[r:r + tq] + total[r + tq:r + 2 * tq]))
    outs, new_carries = [], []
    for log_beta, (after, total), carry, vb in zip(log_betas, sums, carries, vbs):
        logw = log_beta + after
        if carry is not None:
            logw = logw + (carry if tk == LANES else carry[:, :1])
        w = jnp.exp(logw)
        if mask is not None:
            w = jnp.where(mask, w, 0.0)
        if v_transposed:
            outs.append(lax.dot_general(w.astype(BF16), vb, _NT, preferred_element_type=F32))
        else:
            outs.append(jnp.dot(w.astype(BF16), vb, preferred_element_type=F32))
        new_carries.append(total if carry is None else carry + total)
    return outs, new_carries


def _max_all(xs):
    m = xs[0]
    for x in xs[1:]:
        m = jnp.maximum(m, x)
    return jnp.max(m)


def _attn_kernel(*refs, heads, tq, tk, old_blocks, diag_in_old):
    if diag_in_old:
        q_ref, ko_ref, vo_ref, o_ref, acc_ref, carry_ref = refs
    else:
        q_ref, kd_ref, vd_ref, ko_ref, vo_ref, o_ref, acc_ref, carry_ref = refs
    qi = pl.program_id(1)

    def tri(n):
        r = lax.broadcasted_iota(jnp.int32, (n, n), 0)
        c = lax.broadcasted_iota(jnp.int32, (n, n), 1)
        return r, c

    r, c = tri(tq)
    mask = c < r
    upper_d = jnp.where(r > c, 1.0, 0.0).astype(BF16)
    ones_d = jnp.ones((tq, LANES), BF16)
    hs = range(heads)
    qs = [q_ref[0, h] for h in hs]
    if diag_in_old:
        rows = pl.ds(pl.multiple_of(qi * tq, tq), tq)
        kts = [ko_ref[0, h, :, rows] for h in hs]
        vbs = [vo_ref[0, h, rows, :] for h in hs]
    else:
        kts = [kd_ref[0, h] for h in hs]
        vbs = [vd_ref[0, h] for h in hs]
    outs, carries = _sb_block(qs, kts, vbs, [None] * heads, upper_d, ones_d, mask, False)
    for h in hs:
        acc_ref[h] = outs[h]
        carry_ref[h] = carries[h]

    if old_blocks is None:
        j0 = qi - 1
    else:
        j0 = jnp.int32(old_blocks - 1)

    r, c = tri(tk)
    upper_o = jnp.where(r > c, 1.0, 0.0).astype(BF16)
    ones_o = jnp.ones((tk, LANES), BF16)

    def cond(state):
        j, live = state
        return jnp.logical_and(j >= 0, live > DEAD_LOG)

    def body(state):
        j, _ = state
        rows = pl.ds(pl.multiple_of(j * tk, tk), tk)
        qs = [q_ref[0, h] for h in hs]
        kts = [ko_ref[0, h, :, rows].astype(BF16) for h in hs]
        if diag_in_old:
            vbs = [vo_ref[0, h, rows, :] for h in hs]
        else:
            vbs = [vo_ref[0, h, :, rows].astype(BF16) for h in hs]
        old = [carry_ref[h] for h in hs]
        accs = [acc_ref[h] for h in hs]
        outs, new = _sb_block(qs, kts, vbs, old, upper_o, ones_o, None, not diag_in_old)
        for h in hs:
            acc_ref[h] = accs[h] + outs[h]
            carry_ref[h] = new[h]
        return j - 1, _max_all(new)

    lax.while_loop(cond, body, (j0, _max_all(carries)))
    for h in range(heads):
        o_ref[0, :, h * HEAD_DIM:(h + 1) * HEAD_DIM] = acc_ref[h].astype(BF16)


def _attention(q, kt_new, v_new, kt_old=None, vt_old=None, layer=0):
    B, H, S, d = q.shape
    if kt_old is None:
        tq = tk = min(S, ATT_BLOCK)
        assert S % tq == 0
        old_blocks, diag_in_old = None, True
        in_specs = [pl.BlockSpec((1, H, tq, d), lambda b, i: (b, 0, i, 0)),
                    pl.BlockSpec((1, H, d, S), lambda b, i: (b, 0, 0, 0)),
                    pl.BlockSpec((1, H, S, d), lambda b, i: (b, 0, 0, 0))]
        args = (q, kt_new, v_new)
    else:
        P = kt_old.shape[4]
        tq, tk = S, min(P, ATT_BLOCK)
        assert P % tk == 0
        old_blocks, diag_in_old = P // tk, False
        new = pl.BlockSpec((1, H, S, d), lambda b, i: (b, 0, 0, 0))
        new_t = pl.BlockSpec((1, H, d, S), lambda b, i: (b, 0, 0, 0))
        old = pl.BlockSpec((None, 1, H, d, P), lambda b, i: (layer, b, 0, 0, 0))
        in_specs = [new, new_t, new, old, old]
        args = (q, kt_new, v_new, kt_old, vt_old)
    return pl.pallas_call(
        functools.partial(_attn_kernel, heads=H, tq=tq, tk=tk, old_blocks=old_blocks,
                          diag_in_old=diag_in_old),
        grid=(B, S // tq),
        in_specs=in_specs,
        out_specs=pl.BlockSpec((1, tq, H * d), lambda b, i: (b, i, 0)),
        out_shape=jax.ShapeDtypeStruct((B, S, H * d), BF16),
        scratch_shapes=[pltpu.VMEM((H, tq, d), F32), pltpu.VMEM((H, tq, LANES), F32)],
        compiler_params=_params(("parallel", "arbitrary")),
        name="attention",
    )(*args)


def _outproj_kernel(att_ref, c_ref, x_ref, wa_ref, wc_ref, g2_ref, x1_ref, xn_ref):
    x1 = (x_ref[...]
          + jnp.dot(att_ref[...], wa_ref[...], preferred_element_type=F32)
          + jnp.dot(c_ref[...], wc_ref[...], preferred_element_type=F32))
    x1_ref[...] = x1
    ms = jnp.mean(x1 * x1, axis=-1, keepdims=True)
    xn_ref[...] = ((x1 * lax.rsqrt(ms + EPS)) * g2_ref[...]).astype(BF16)


def _outproj(att, c, x, wa, wc, g2):
    N, D = x.shape
    T = min(N, ROW_TILE)
    assert N % T == 0
    row = lambda w: pl.BlockSpec((T, w), lambda i: (i, 0))
    return pl.pallas_call(
        _outproj_kernel,
        grid=(N // T,),
        in_specs=[row(att.shape[1]), row(c.shape[1]), row(D),
                  _const_spec(wa.shape), _const_spec(wc.shape), _const_spec(g2.shape)],
        out_specs=[row(D), row(D)],
        out_shape=[jax.ShapeDtypeStruct((N, D), F32), jax.ShapeDtypeStruct((N, D), BF16)],
        compiler_params=_params(("parallel",)),
        name="outproj",
    )(att, c, x, wa, wc, g2)


def _candidate_pairs():
    return [(p, q) for p in range(1, PEER_TOPK + 1) for q in range(1, PEER_TOPK + 1)
            if p * q <= PEER_TOPK]


def _sort_desc(v):
    v = list(v)
    n = len(v)
    k = 2
    while k <= n:
        j = k // 2
        while j >= 1:
            for i in range(n):
                m = i ^ j
                if m > i:
                    hi, lo = jnp.maximum(v[i], v[m]), jnp.minimum(v[i], v[m])
                    v[i], v[m] = (hi, lo) if (i & k) == 0 else (lo, hi)
            j //= 2
        k *= 2
    return v


def _merge_top(a, b):
    n = len(a)
    c = [jnp.maximum(a[i], b[n - 1 - i]) for i in range(n)]
    j = n // 2
    while j >= 1:
        for i in range(n):
            m = i ^ j
            if m > i:
                c[i], c[m] = jnp.maximum(c[i], c[m]), jnp.minimum(c[i], c[m])
        j //= 2
    return c


def _top_values(load, count):
    k = PEER_TOPK
    groups = []
    for g in range(0, count, k):
        vals = [load(n) for n in range(g, min(g + k, count))]
        vals += [jnp.full_like(vals[0], -jnp.inf)] * (k - len(vals))
        groups.append(_sort_desc(vals))
    while len(groups) > 1:
        merged = [_merge_top(groups[i], groups[i + 1]) for i in range(0, len(groups) - 1, 2)]
        groups = merged + ([groups[-1]] if len(groups) % 2 else [])
    return groups[0]


def _peer_route(xn_ref, wq_ref, kb_ref, xnt_ref, s_ref, top_ref, cand_ref, kth_ref,
                r0_ref, e0_ref, cnt_ref, e1_ref, *, nk, ph, tokens):
    xnt_ref[...] = xn_ref[...].astype(F32).T.astype(BF16)
    xnt = xnt_ref[...]
    for c in range(2):
        qt = jnp.dot(wq_ref[c], xnt, preferred_element_type=F32)
        sc = jnp.dot(kb_ref[c], qt.astype(BF16), preferred_element_type=F32)
        for lt in range(tokens // LANES):
            s_ref[c, lt] = sc[:, lt * LANES:(lt + 1) * LANES]

    pairs = _candidate_pairs()
    for lt in range(tokens // LANES):
        for c in range(2):
            top = _top_values(lambda n: s_ref[c, lt, n * ph:(n + 1) * ph, :], nk)
            for p in range(PEER_TOPK):
                top_ref[c, p, lt] = top[p]
        for n, (p, q) in enumerate(pairs):
            cand_ref[n, lt] = top_ref[0, p - 1, lt] + top_ref[1, q - 1, lt]
        kth_ref[lt] = _top_values(lambda n: cand_ref[n, lt], len(pairs))[PEER_TOPK - 1]
    theta = kth_ref[...]
    a1 = top_ref[0, 0]
    b1 = top_ref[1, 0]
    best = a1 + b1
    denom = None
    for n in range(len(pairs)):
        cv = cand_ref[n]
        t = jnp.where(cv >= theta, jnp.exp(cv - best), 0.0)
        denom = t if denom is None else denom + t
    inv = 1.0 / denom

    def first_half(i, _):
        rows = pl.ds(pl.multiple_of(i * ph, ph), ph)
        s0 = s_ref[0, :, rows, :]
        rank = jnp.ones_like(s0)
        for p in range(PEER_TOPK):
            rank = jnp.where(top_ref[0, p] > s0, p + 2.0, rank)
        r0_ref[:, rows, :] = _bf16_pair(rank)
        e0_ref[:, rows, :] = _bf16_pair(jnp.exp(s0 - a1) * inv)
        return 0
    lax.fori_loop(0, nk, first_half, 0)

    def second_half(j, _):
        rows = pl.ds(pl.multiple_of(j * ph, ph), ph)
        s1 = s_ref[1, :, rows, :]
        cnt = jnp.zeros_like(s1)
        for p in range(PEER_TOPK):
            cnt = jnp.where(top_ref[0, p] + s1 >= theta, p + 1.0, cnt)
        s_ref[1, :, rows, :] = cnt
        s_ref[0, :, rows, :] = jnp.exp(s1 - b1)
        return 0
    lax.fori_loop(0, nk, second_half, 0)
    for lt in range(tokens // LANES):
        for h in range(ph):
            cnt_ref[lt, h * nk:(h + 1) * nk, :] = (
                s_ref[1, lt, pl.ds(h, nk, stride=ph), :].astype(BF16))
            e1_ref[lt, h * nk:(h + 1) * nk, :] = (
                s_ref[0, lt, pl.ds(h, nk, stride=ph), :].astype(BF16))


def _bf16_pair(x):
    bits = pltpu.bitcast(x.astype(BF16).astype(F32), jnp.uint32)
    return bits | lax.shift_right_logical(bits, jnp.uint32(16))


def _zero_bits_after(x):
    bits = pltpu.bitcast(x, jnp.uint32)
    return lax.shift_right_logical(lax.shift_right_logical(bits, jnp.uint32(16)), jnp.uint32(16))


def _peer_gate_chunk(c, xu, coef_ref, r0_ref, e0_ref, cnt_ref, e1_ref, *, nk, ph, chunk, tokens):
    half_tile = (PEER_SUB // 2, LANES)

    for ib in range(chunk // nk):
        i = c * (chunk // nk) + ib
        hrows = pl.ds(pl.multiple_of(i * ph, ph), ph)
        row0 = ib * nk
        for lt in range(tokens // LANES):
            lanes = slice(lt * LANES, (lt + 1) * LANES)
            zero = _zero_bits_after(xu[row0:row0 + SUBLANES, lanes])
            rank = r0_ref[lt, hrows, :] | zero
            e0 = e0_ref[lt, hrows, :] | zero
            w = [None] * (nk // PEER_SUB)
            for h in range(ph):
                if h:
                    zero = _zero_bits_after(pltpu.bitcast(w[-1], jnp.uint32))
                rank_b = pltpu.bitcast(jnp.broadcast_to(rank[h:h + 1], half_tile) | zero, BF16)
                e0_b = pltpu.bitcast(jnp.broadcast_to(e0[h:h + 1], half_tile) | zero, BF16)
                for jb in range(nk // PEER_SUB):
                    jrows = slice(h * nk + jb * PEER_SUB, h * nk + (jb + 1) * PEER_SUB)
                    t = jnp.where(rank_b <= cnt_ref[lt, jrows, :],
                                  e1_ref[lt, jrows, :], 0.0) * e0_b
                    w[jb] = t if w[jb] is None else w[jb] + t
            for jb in range(nk // PEER_SUB):
                erows = slice(row0 + jb * PEER_SUB, row0 + (jb + 1) * PEER_SUB)
                xs = xu[erows, lanes].astype(BF16)
                act = xs * (0.5 + 0.5 * lax.erf(xs * math.sqrt(0.5)))
                coef_ref[erows, lanes] = w[jb] * act


def _peer_kernel(xn_ref, x1_ref, wq_ref, kb_ref, u_ref, vt_ref, y_ref,
                 s_ref, top_ref, cand_ref, kth_ref, r0_ref, e0_ref, cnt_ref, e1_ref,
                 acc_ref, coef_ref, xnt_ref, *, nk, ph, chunk, tokens):
    e = pl.program_id(1)

    @pl.when(e == 0)
    def _():
        _peer_route(xn_ref, wq_ref, kb_ref, xnt_ref, s_ref, top_ref, cand_ref, kth_ref,
                    r0_ref, e0_ref, cnt_ref, e1_ref, nk=nk, ph=ph, tokens=tokens)
        acc_ref[...] = jnp.zeros_like(acc_ref)

    xu = jnp.dot(u_ref[...], xnt_ref[...], preferred_element_type=F32)
    _peer_gate_chunk(e, xu, coef_ref, r0_ref, e0_ref, cnt_ref, e1_ref,
                     nk=nk, ph=ph, chunk=chunk, tokens=tokens)
    acc_ref[...] += jnp.dot(vt_ref[...], coef_ref[...], preferred_element_type=F32)

    @pl.when(e == pl.num_programs(1) - 1)
    def _():
        y_ref[...] = x1_ref[...] + acc_ref[...].T


def _peer(xn, x1, wq_t, kbig, u_bf, vt_bf):
    N, D = xn.shape
    E = u_bf.shape[0]
    ph = PEER_HEADS
    nk = kbig.shape[1] // ph
    T = min(N, PEER_TOKENS)
    chunk = min(E, PEER_EXPERT_CHUNK)
    assert N % T == 0 and E % chunk == 0 and chunk % nk == 0 and T % LANES == 0
    npairs = len(_candidate_pairs())
    lt = T // LANES
    once = dict(pipeline_mode=pl.Buffered(1))
    return pl.pallas_call(
        functools.partial(_peer_kernel, nk=nk, ph=ph, chunk=chunk, tokens=T),
        grid=(N // T, E // chunk),
        in_specs=[pl.BlockSpec((T, D), lambda t, e: (t, 0)),
                  pl.BlockSpec((T, D), lambda t, e: (t, 0), **once),
                  pl.BlockSpec(wq_t.shape, lambda t, e: (0, 0, 0), **once),
                  pl.BlockSpec(kbig.shape, lambda t, e: (0, 0, 0), **once),
                  pl.BlockSpec((chunk, D), lambda t, e: (e, 0)),
                  pl.BlockSpec((D, chunk), lambda t, e: (0, e))],
        out_specs=pl.BlockSpec((T, D), lambda t, e: (t, 0)),
        out_shape=jax.ShapeDtypeStruct((N, D), F32),
        scratch_shapes=[pltpu.VMEM((2, lt, nk * ph, LANES), F32),
                        pltpu.VMEM((2, PEER_TOPK, lt, ph, LANES), F32),
                        pltpu.VMEM((npairs, lt, ph, LANES), F32),
                        pltpu.VMEM((lt, ph, LANES), F32),
                        pltpu.VMEM((lt, nk * ph, LANES), jnp.uint32),
                        pltpu.VMEM((lt, nk * ph, LANES), jnp.uint32),
                        pltpu.VMEM((lt, ph * nk, LANES), BF16),
                        pltpu.VMEM((lt, ph * nk, LANES), BF16),
                        pltpu.VMEM((D, T), F32),
                        pltpu.VMEM((chunk, T), BF16),
                        pltpu.VMEM((D, T), BF16)],
        compiler_params=_params(("parallel", "arbitrary")),
        name="peer",
    )(xn, x1, wq_t, kbig, u_bf, vt_bf)


def _layer_weights(norm1_g, w_in, q_norm_g, k_norm_g, conv_w, conv_b, conv_ln_g, conv_ln_b,
                   w_out, norm2_g, peer_wq, peer_keys, peer_u, peer_v):
    att_w = (w_in.shape[1] - 2 * conv_w.shape[1]) // 3
    heads = att_w // HEAD_DIM
    row = lambda v: v.reshape(1, -1)
    head_of = jnp.arange(att_w) // HEAD_DIM
    gm = jnp.where(head_of[:, None] == head_of[None, :], 1.0 / HEAD_DIM, 0.0).astype(BF16)
    _, nk, half = peer_keys.shape
    ph = PEER_HEADS
    D = peer_wq.shape[0]
    wq_t = peer_wq.reshape(D, ph, 2, half).transpose(2, 1, 3, 0).reshape(2, ph * half, D)
    eye = jnp.eye(ph, dtype=peer_keys.dtype)
    kbig = (peer_keys[:, :, None, None, :] * eye[None, None, :, :, None]
            ).reshape(2, nk * ph, ph * half)
    return dict(
        g1=row(norm1_g), w_in=w_in.astype(BF16),
        gq=row(jnp.tile(q_norm_g, heads)), gk=row(jnp.tile(k_norm_g, heads)), gm=gm,
        cw=conv_w, cb=row(conv_b), lg=row(conv_ln_g), lb=row(conv_ln_b),
        wa=w_out[:att_w].astype(BF16), wc=w_out[att_w:].astype(BF16), g2=row(norm2_g),
        wq_t=wq_t.astype(BF16), kbig=kbig.astype(BF16),
        u=peer_u.astype(BF16), vt=peer_v.T.astype(BF16))


def _layer(x, caches, conv_hist, w, layer):
    B, S, D = x.shape
    weights = (w["g1"], w["w_in"], w["gq"], w["gk"], w["gm"])
    if caches is None:
        q, kt, vt, kt_bf, v_bf, glu = _inproj(x, *weights, transposed=True)
        att = _attention(q, kt_bf, v_bf)
        k, v = jnp.swapaxes(kt, -1, -2), jnp.swapaxes(vt, -1, -2)
        conv_hist = jnp.zeros((B, w["cw"].shape[0] - 1, glu.shape[-1]), x.dtype)
    else:
        q, k, v, glu = _inproj(x, *weights, transposed=False)
        kt_new = jnp.swapaxes(k, -1, -2).astype(BF16)
        att = _attention(q, kt_new, v.astype(BF16), *caches, layer=layer)
    c, new_conv = _conv(glu, conv_hist, w["cw"], w["cb"], w["lg"], w["lb"])
    x1, xn = _outproj(att.reshape(B * S, -1), c.reshape(B * S, -1), x.reshape(B * S, D),
                      w["wa"], w["wc"], w["g2"])
    y = _peer(xn, x1, w["wq_t"], w["kbig"], w["u"], w["vt"])
    return y.reshape(B, S, D), k, v, new_conv


def kernel(x_prompt, x_sample, cache_k, cache_v, state_conv, norm1_g, w_in, q_norm_g, k_norm_g,
           conv_w, conv_b, conv_ln_g, conv_ln_b, w_out, norm2_g, peer_wq, peer_keys, peer_u,
           peer_v):
    y_p, y_s = x_prompt, x_sample
    cache_k = jnp.swapaxes(cache_k, -1, -2)
    cache_v = jnp.swapaxes(cache_v, -1, -2)
    outs = [[] for _ in range(6)]
    for l in range(norm1_g.shape[0]):
        w = _layer_weights(norm1_g[l], w_in[l], q_norm_g[l], k_norm_g[l], conv_w[l], conv_b[l],
                           conv_ln_g[l], conv_ln_b[l], w_out[l], norm2_g[l], peer_wq[l],
                           peer_keys[l], peer_u[l], peer_v[l])
        y_p, kp, vp, cp = _layer(y_p, None, None, w, l)
        y_s, ks, vs, cs = _layer(y_s, (cache_k, cache_v), state_conv[l], w, l)
        for lst, val in zip(outs, (kp, vp, cp, ks, vs, cs)):
            lst.append(val)
    return (y_p, y_s) + tuple(jnp.stack(o) if len(o) > 1 else o[0][None] for o in outs)
```

```python
import functools
import math

import jax
import jax.numpy as jnp
from jax import lax
from jax.experimental import pallas as pl
from jax.experimental.pallas import tpu as pltpu

F32 = jnp.float32
BF16 = jnp.bfloat16

EPS = 1e-6
HEAD_DIM = 64
PEER_HEADS = 8
PEER_TOPK = 16
LANES = 128
SUBLANES = 8
ROW_TILE = 512
CONV_ROWS = 64
ATT_BLOCK = 128
PEER_TOKENS = 512
PEER_EXPERT_CHUNK = 2048
PEER_SUB = 16
VMEM_LIMIT = 60 * 1024 * 1024
DEAD_LOG = -120.0

_NT = (((1,), (1,)), ((), ()))


def _const_spec(shape):
    zeros = (0,) * len(shape)
    return pl.BlockSpec(shape, lambda *_: zeros)


def _params(semantics):
    return pltpu.CompilerParams(dimension_semantics=semantics, vmem_limit_bytes=VMEM_LIMIT)


def _inproj_kernel(x_ref, g1_ref, w_ref, gq_ref, gk_ref, gm_ref, q_ref, *out_refs,
                   heads, att_w, conv_ch, transposed):
    x = x_ref[0]
    ms = jnp.mean(x * x, axis=-1, keepdims=True)
    hb = ((x * lax.rsqrt(ms + EPS)) * g1_ref[...]).astype(BF16)

    def proj(c0, width):
        return jnp.dot(hb, w_ref[:, c0:c0 + width], preferred_element_type=F32)

    def head_norm(t, g):
        msq = jnp.dot((t * t).astype(BF16), gm_ref[...], preferred_element_type=F32)
        return (t * lax.rsqrt(msq + EPS)) * g

    qn = head_norm(proj(0, att_w), gq_ref[...]) * (HEAD_DIM ** -0.5)
    kn = head_norm(proj(att_w, att_w), gk_ref[...])
    v = proj(2 * att_w, att_w)
    if transposed:
        kt_ref, vt_ref, ktb_ref, vb_ref, glu_ref = out_refs
        knt, vt = kn.T, v.T
    else:
        k_ref, v_ref, glu_ref = out_refs
    for h in range(heads):
        sl = slice(h * HEAD_DIM, (h + 1) * HEAD_DIM)
        q_ref[0, h] = qn[:, sl].astype(BF16)
        if transposed:
            kt_ref[0, h] = knt[sl, :]
            vt_ref[0, h] = vt[sl, :]
            ktb_ref[0, h] = knt[sl, :].astype(BF16)
            vb_ref[0, h] = v[:, sl].astype(BF16)
        else:
            k_ref[0, h] = kn[:, sl]
            v_ref[0, h] = v[:, sl]
    a = proj(3 * att_w, conv_ch)
    gate = proj(3 * att_w + conv_ch, conv_ch)
    glu_ref[0] = a * jax.nn.sigmoid(gate)


def _inproj(x, g1, w_bf, gq, gk, gm, transposed):
    B, S, D = x.shape
    att_w = gq.shape[-1]
    heads = att_w // HEAD_DIM
    conv_ch = (w_bf.shape[1] - 3 * att_w) // 2
    T = min(S, ROW_TILE)
    assert S % T == 0
    hm = lambda dt: jax.ShapeDtypeStruct((B, heads, S, HEAD_DIM), dt)
    hspec = pl.BlockSpec((1, heads, T, HEAD_DIM), lambda b, s: (b, 0, s, 0))
    tm = lambda dt: jax.ShapeDtypeStruct((B, heads, HEAD_DIM, S), dt)
    tspec = pl.BlockSpec((1, heads, HEAD_DIM, T), lambda b, s: (b, 0, 0, s))
    if transposed:
        kv_specs, kv_shapes = [tspec, tspec, tspec, hspec], [tm(F32), tm(F32), tm(BF16), hm(BF16)]
    else:
        kv_specs, kv_shapes = [hspec, hspec], [hm(F32), hm(F32)]
    return pl.pallas_call(
        functools.partial(_inproj_kernel, heads=heads, att_w=att_w, conv_ch=conv_ch,
                          transposed=transposed),
        grid=(B, S // T),
        in_specs=[pl.BlockSpec((1, T, D), lambda b, s: (b, s, 0)),
                  _const_spec(g1.shape), _const_spec(w_bf.shape),
                  _const_spec(gq.shape), _const_spec(gk.shape), _const_spec(gm.shape)],
        out_specs=[hspec] + kv_specs + [pl.BlockSpec((1, T, conv_ch), lambda b, s: (b, s, 0))],
        out_shape=[hm(BF16)] + kv_shapes + [jax.ShapeDtypeStruct((B, S, conv_ch), F32)],
        compiler_params=_params(("parallel", "parallel")),
        name="inproj",
    )(x, g1, w_bf, gq, gk, gm)


def _conv_kernel(glu_ref, hist_ref, w_ref, b_ref, lg_ref, lb_ref, c_ref, new_ref, ext_ref,
                 *, rows, width, chunk):
    hist = width - 1
    base = 32
    off = base - hist
    s = pl.program_id(1)

    @pl.when(s == 0)
    def _():
        ext_ref[off:base, :] = hist_ref[0]
        ext_ref[base + rows:, :] = jnp.zeros((SUBLANES, ext_ref.shape[1]), F32)

    ext_ref[base:base + rows, :] = glu_ref[0]
    span = chunk + SUBLANES
    lanes_all = ext_ref.shape[1]
    for r0 in range(0, rows, chunk):
        cols = []
        for l0 in range(0, lanes_all, LANES):
            acc = None
            for b in range(SUBLANES):
                part = None
                for a in range((off + width - 1) // SUBLANES + 1):
                    w = SUBLANES * a + b - off
                    if 0 <= w < width:
                        term = (ext_ref[r0 + SUBLANES * a:r0 + SUBLANES * a + span, l0:l0 + LANES]
                                * w_ref[w:w + 1, l0:l0 + LANES])
                        part = term if part is None else part + term
                if part is not None:
                    acc = part[b:b + chunk] if acc is None else acc + part[b:b + chunk]
            cols.append(acc)
        acc = jnp.concatenate(cols, axis=1) + b_ref[...]
        mu = jnp.mean(acc, axis=-1, keepdims=True)
        d = acc - mu
        var = jnp.mean(d * d, axis=-1, keepdims=True)
        y = d * lax.rsqrt(var + EPS) * lg_ref[...] + lb_ref[...]
        c_ref[0, r0:r0 + chunk, :] = (y * jax.nn.sigmoid(y)).astype(BF16)
    tail = ext_ref[rows + off:rows + base, :]
    new_ref[0] = tail
    ext_ref[off:base, :] = tail


def _conv(glu, hist, cw, cb, lg, lb):
    B, S, C = glu.shape
    width = cw.shape[0]
    T = min(S, ROW_TILE)
    assert S % T == 0 and (T >= width - 1 or S == T)
    chunk = min(T, CONV_ROWS)
    return pl.pallas_call(
        functools.partial(_conv_kernel, rows=T, width=width, chunk=chunk),
        grid=(B, S // T),
        in_specs=[pl.BlockSpec((1, T, C), lambda b, s: (b, s, 0)),
                  pl.BlockSpec((1, width - 1, C), lambda b, s: (b, 0, 0)),
                  _const_spec(cw.shape), _const_spec(cb.shape),
                  _const_spec(lg.shape), _const_spec(lb.shape)],
        out_specs=[pl.BlockSpec((1, T, C), lambda b, s: (b, s, 0)),
                   pl.BlockSpec((1, width - 1, C), lambda b, s: (b, 0, 0))],
        out_shape=[jax.ShapeDtypeStruct((B, S, C), BF16),
                   jax.ShapeDtypeStruct((B, width - 1, C), F32)],
        scratch_shapes=[pltpu.VMEM((32 + T + SUBLANES, C), F32)],
        compiler_params=_params(("parallel", "arbitrary")),
        name="conv",
    )(glu, hist, cw, cb, lg, lb)


def _sb_block(qs, kts, vbs, carries, upper, ones, mask, v_transposed):
    tq, tk = qs[0].shape[0], kts[0].shape[1]
    zs = [jnp.dot(q, kt, preferred_element_type=F32) for q, kt in zip(qs, kts)]
    log_betas, terms = [], []
    for z in zs:
        sp = jnp.log1p(jnp.exp(-jnp.abs(z)))
        log_beta = jnp.minimum(z, 0.0) - sp
        log_1m = log_beta - z
        if mask is not None:
            log_1m = jnp.where(mask, log_1m, 0.0)
        hi = log_1m.astype(BF16)
        lo = (log_1m - hi.astype(F32)).astype(BF16)
        log_betas.append(log_beta)
        terms += [hi, lo]
    both = jnp.concatenate(terms, axis=0)
    if tk == LANES:
        p = jnp.dot(both, jnp.concatenate([upper, ones], axis=1), preferred_element_type=F32)
        after, total = p[:, :tk], p[:, tk:]
    else:
        after = jnp.dot(both, upper, preferred_element_type=F32)
        total = jnp.dot(both, ones, preferred_element_type=F32)
    sums = []
    for h in range(len(zs)):
        r = 2 * h * tq
        sums.append((after[r:r + tq] + after[r + tq:r + 2 * tq],
                     total[r:r + tq] + total[r + tq:r + 2 * tq]))
    outs, new_carries = [], []
    for log_beta, (after, total), carry, vb in zip(log_betas, sums, carries, vbs):
        logw = log_beta + after
        if carry is not None:
            logw = logw + (carry if tk == LANES else carry[:, :1])
        w = jnp.exp(logw)
        if mask is not None:
            w = jnp.where(mask, w, 0.0)
        if v_transposed:
            outs.append(lax.dot_general(w.astype(BF16), vb, _NT, preferred_element_type=F32))
        else:
            outs.append(jnp.dot(w.astype(BF16), vb, preferred_element_type=F32))
        new_carries.append(total if carry is None else carry + total)
    return outs, new_carries


def _max_all(xs):
    m = xs[0]
    for x in xs[1:]:
        m = jnp.maximum(m, x)
    return jnp.max(m)


def _attn_kernel(*refs, heads, tq, tk, old_blocks, diag_in_old):
    if diag_in_old:
        q_ref, ko_ref, vo_ref, o_ref, acc_ref, carry_ref = refs
    else:
        q_ref, kd_ref, vd_ref, ko_ref, vo_ref, o_ref, acc_ref, carry_ref = refs
    qi = pl.program_id(1)

    def tri(n):
        r = lax.broadcasted_iota(jnp.int32, (n, n), 0)
        c = lax.broadcasted_iota(jnp.int32, (n, n), 1)
        return r, c

    r, c = tri(tq)
    mask = c < r
    upper_d = jnp.where(r > c, 1.0, 0.0).astype(BF16)
    ones_d = jnp.ones((tq, LANES), BF16)
    hs = range(heads)
    qs = [q_ref[0, h] for h in hs]
    if diag_in_old:
        rows = pl.ds(pl.multiple_of(qi * tq, tq), tq)
        kts = [ko_ref[0, h, :, rows] for h in hs]
        vbs = [vo_ref[0, h, rows, :] for h in hs]
    else:
        kts = [kd_ref[0, h] for h in hs]
        vbs = [vd_ref[0, h] for h in hs]
    outs, carries = _sb_block(qs, kts, vbs, [None] * heads, upper_d, ones_d, mask, False)
    for h in hs:
        acc_ref[h] = outs[h]
        carry_ref[h] = carries[h]

    if old_blocks is None:
        j0 = qi - 1
    else:
        j0 = jnp.int32(old_blocks - 1)

    r, c = tri(tk)
    upper_o = jnp.where(r > c, 1.0, 0.0).astype(BF16)
    ones_o = jnp.ones((tk, LANES), BF16)

    def cond(state):
        j, live = state
        return jnp.logical_and(j >= 0, live > DEAD_LOG)

    def body(state):
        j, _ = state
        rows = pl.ds(pl.multiple_of(j * tk, tk), tk)
        qs = [q_ref[0, h] for h in hs]
        kts = [ko_ref[0, h, :, rows].astype(BF16) for h in hs]
        if diag_in_old:
            vbs = [vo_ref[0, h, rows, :] for h in hs]
        else:
            vbs = [vo_ref[0, h, :, rows].astype(BF16) for h in hs]
        old = [carry_ref[h] for h in hs]
        accs = [acc_ref[h] for h in hs]
        outs, new = _sb_block(qs, kts, vbs, old, upper_o, ones_o, None, not diag_in_old)
        for h in hs:
            acc_ref[h] = accs[h] + outs[h]
            carry_ref[h] = new[h]
        return j - 1, _max_all(new)

    lax.while_loop(cond, body, (j0, _max_all(carries)))
    for h in range(heads):
        o_ref[0, :, h * HEAD_DIM:(h + 1) * HEAD_DIM] = acc_ref[h].astype(BF16)


def _attention(q, kt_new, v_new, kt_old=None, vt_old=None, layer=0):
    B, H, S, d = q.shape
    if kt_old is None:
        tq = tk = min(S, ATT_BLOCK)
        assert S % tq == 0
        old_blocks, diag_in_old = None, True
        in_specs = [pl.BlockSpec((1, H, tq, d), lambda b, i: (b, 0, i, 0)),
                    pl.BlockSpec((1, H, d, S), lambda b, i: (b, 0, 0, 0)),
                    pl.BlockSpec((1, H, S, d), lambda b, i: (b, 0, 0, 0))]
        args = (q, kt_new, v_new)
    else:
        P = kt_old.shape[4]
        tq, tk = S, min(P, ATT_BLOCK)
        assert P % tk == 0
        old_blocks, diag_in_old = P // tk, False
        new = pl.BlockSpec((1, H, S, d), lambda b, i: (b, 0, 0, 0))
        new_t = pl.BlockSpec((1, H, d, S), lambda b, i: (b, 0, 0, 0))
        old = pl.BlockSpec((None, 1, H, d, P), lambda b, i: (layer, b, 0, 0, 0))
        in_specs = [new, new_t, new, old, old]
        args = (q, kt_new, v_new, kt_old, vt_old)
    return pl.pallas_call(
        functools.partial(_attn_kernel, heads=H, tq=tq, tk=tk, old_blocks=old_blocks,
                          diag_in_old=diag_in_old),
        grid=(B, S // tq),
        in_specs=in_specs,
        out_specs=pl.BlockSpec((1, tq, H * d), lambda b, i: (b, i, 0)),
        out_shape=jax.ShapeDtypeStruct((B, S, H * d), BF16),
        scratch_shapes=[pltpu.VMEM((H, tq, d), F32), pltpu.VMEM((H, tq, LANES), F32)],
        compiler_params=_params(("parallel", "arbitrary")),
        name="attention",
    )(*args)


def _outproj_kernel(att_ref, c_ref, x_ref, wa_ref, wc_ref, g2_ref, x1_ref, xn_ref):
    x1 = (x_ref[...]
          + jnp.dot(att_ref[...], wa_ref[...], preferred_element_type=F32)
          + jnp.dot(c_ref[...], wc_ref[...], preferred_element_type=F32))
    x1_ref[...] = x1
    ms = jnp.mean(x1 * x1, axis=-1, keepdims=True)
    xn_ref[...] = ((x1 * lax.rsqrt(ms + EPS)) * g2_ref[...]).astype(BF16)


def _outproj(att, c, x, wa, wc, g2):
    N, D = x.shape
    T = min(N, ROW_TILE)
    assert N % T == 0
    row = lambda w: pl.BlockSpec((T, w), lambda i: (i, 0))
    return pl.pallas_call(
        _outproj_kernel,
        grid=(N // T,),
        in_specs=[row(att.shape[1]), row(c.shape[1]), row(D),
                  _const_spec(wa.shape), _const_spec(wc.shape), _const_spec(g2.shape)],
        out_specs=[row(D), row(D)],
        out_shape=[jax.ShapeDtypeStruct((N, D), F32), jax.ShapeDtypeStruct((N, D), BF16)],
        compiler_params=_params(("parallel",)),
        name="outproj",
    )(att, c, x, wa, wc, g2)


def _candidate_pairs():
    return [(p, q) for p in range(1, PEER_TOPK + 1) for q in range(1, PEER_TOPK + 1)
            if p * q <= PEER_TOPK]


def _sort_desc(v):
    v = list(v)
    n = len(v)
    k = 2
    while k <= n:
        j = k // 2
        while j >= 1:
            for i in range(n):
                m = i ^ j
                if m > i:
                    hi, lo = jnp.maximum(v[i], v[m]), jnp.minimum(v[i], v[m])
                    v[i], v[m] = (hi, lo) if (i & k) == 0 else (lo, hi)
            j //= 2
        k *= 2
    return v


def _merge_top(a, b):
    n = len(a)
    c = [jnp.maximum(a[i], b[n - 1 - i]) for i in range(n)]
    j = n // 2
    while j >= 1:
        for i in range(n):
            m = i ^ j
            if m > i:
                c[i], c[m] = jnp.maximum(c[i], c[m]), jnp.minimum(c[i], c[m])
        j //= 2
    return c


def _top_values(load, count):
    k = PEER_TOPK
    groups = []
    for g in range(0, count, k):
        vals = [load(n) for n in range(g, min(g + k, count))]
        vals += [jnp.full_like(vals[0], -jnp.inf)] * (k - len(vals))
        groups.append(_sort_desc(vals))
    while len(groups) > 1:
        merged = [_merge_top(groups[i], groups[i + 1]) for i in range(0, len(groups) - 1, 2)]
        groups = merged + ([groups[-1]] if len(groups) % 2 else [])
    return groups[0]


def _peer_route(xn_ref, wq_ref, kb_ref, xnt_ref, s_ref, top_ref, cand_ref, kth_ref,
                r0_ref, e0_ref, cnt_ref, e1_ref, *, nk, ph, tokens):
    xnt_ref[...] = xn_ref[...].astype(F32).T.astype(BF16)
    xnt = xnt_ref[...]
    for c in range(2):
        qt = jnp.dot(wq_ref[c], xnt, preferred_element_type=F32)
        sc = jnp.dot(kb_ref[c], qt.astype(BF16), preferred_element_type=F32)
        for lt in range(tokens // LANES):
            s_ref[c, lt] = sc[:, lt * LANES:(lt + 1) * LANES]

    pairs = _candidate_pairs()
    for lt in range(tokens // LANES):
        for c in range(2):
            top = _top_values(lambda n: s_ref[c, lt, n * ph:(n + 1) * ph, :], nk)
            for p in range(PEER_TOPK):
                top_ref[c, p, lt] = top[p]
        for n, (p, q) in enumerate(pairs):
            cand_ref[n, lt] = top_ref[0, p - 1, lt] + top_ref[1, q - 1, lt]
        kth_ref[lt] = _top_values(lambda n: cand_ref[n, lt], len(pairs))[PEER_TOPK - 1]
    theta = kth_ref[...]
    a1 = top_ref[0, 0]
    b1 = top_ref[1, 0]
    best = a1 + b1
    denom = None
    for n in range(len(pairs)):
        cv = cand_ref[n]
        t = jnp.where(cv >= theta, jnp.exp(cv - best), 0.0)
        denom = t if denom is None else denom + t
    inv = 1.0 / denom

    def first_half(i, _):
        rows = pl.ds(pl.multiple_of(i * ph, ph), ph)
        s0 = s_ref[0, :, rows, :]
        rank = jnp.ones_like(s0)
        for p in range(PEER_TOPK):
            rank = jnp.where(top_ref[0, p] > s0, p + 2.0, rank)
        r0_ref[:, rows, :] = _bf16_pair(rank)
        e0_ref[:, rows, :] = _bf16_pair(jnp.exp(s0 - a1) * inv)
        return 0
    lax.fori_loop(0, nk, first_half, 0)

    def second_half(j, _):
        rows = pl.ds(pl.multiple_of(j * ph, ph), ph)
        s1 = s_ref[1, :, rows, :]
        cnt = jnp.zeros_like(s1)
        for p in range(PEER_TOPK):
            cnt = jnp.where(top_ref[0, p] + s1 >= theta, p + 1.0, cnt)
        s_ref[1, :, rows, :] = cnt
        s_ref[0, :, rows, :] = jnp.exp(s1 - b1)
        return 0
    lax.fori_loop(0, nk, second_half, 0)
    for lt in range(tokens // LANES):
        for h in range(ph):
            cnt_ref[lt, h * nk:(h + 1) * nk, :] = (
                s_ref[1, lt, pl.ds(h, nk, stride=ph), :].astype(BF16))
            e1_ref[lt, h * nk:(h + 1) * nk, :] = (
                s_ref[0, lt, pl.ds(h, nk, stride=ph), :].astype(BF16))


def _bf16_pair(x):
    bits = pltpu.bitcast(x.astype(BF16).astype(F32), jnp.uint32)
    return bits | lax.shift_right_logical(bits, jnp.uint32(16))


def _zero_bits_after(x):
    bits = pltpu.bitcast(x, jnp.uint32)
    return lax.shift_right_logical(lax.shift_right_logical(bits, jnp.uint32(16)), jnp.uint32(16))


def _peer_gate_chunk(c, xu, coef_ref, r0_ref, e0_ref, cnt_ref, e1_ref, *, nk, ph, chunk, tokens):
    half_tile = (PEER_SUB // 2, LANES)

    for ib in range(chunk // nk):
        i = c * (chunk // nk) + ib
        hrows = pl.ds(pl.multiple_of(i * ph, ph), ph)
        row0 = ib * nk
        for lt in range(tokens // LANES):
            lanes = slice(lt * LANES, (lt + 1) * LANES)
            zero = _zero_bits_after(xu[row0:row0 + SUBLANES, lanes])
            rank = r0_ref[lt, hrows, :] | zero
            e0 = e0_ref[lt, hrows, :] | zero
            w = [None] * (nk // PEER_SUB)
            for h in range(ph):
                if h:
                    zero = _zero_bits_after(pltpu.bitcast(w[-1], jnp.uint32))
                rank_b = pltpu.bitcast(jnp.broadcast_to(rank[h:h + 1], half_tile) | zero, BF16)
                e0_b = pltpu.bitcast(jnp.broadcast_to(e0[h:h + 1], half_tile) | zero, BF16)
                for jb in range(nk // PEER_SUB):
                    jrows = slice(h * nk + jb * PEER_SUB, h * nk + (jb + 1) * PEER_SUB)
                    t = jnp.where(rank_b <= cnt_ref[lt, jrows, :],
                                  e1_ref[lt, jrows, :], 0.0) * e0_b
                    w[jb] = t if w[jb] is None else w[jb] + t
            for jb in range(nk // PEER_SUB):
                erows = slice(row0 + jb * PEER_SUB, row0 + (jb + 1) * PEER_SUB)
                xs = xu[erows, lanes].astype(BF16)
                act = xs * (0.5 + 0.5 * lax.erf(xs * math.sqrt(0.5)))
                coef_ref[erows, lanes] = w[jb] * act


def _peer_kernel(xn_ref, x1_ref, wq_ref, kb_ref, u_ref, vt_ref, y_ref,
                 s_ref, top_ref, cand_ref, kth_ref, r0_ref, e0_ref, cnt_ref, e1_ref,
                 acc_ref, coef_ref, xnt_ref, *, nk, ph, chunk, tokens):
    e = pl.program_id(1)

    @pl.when(e == 0)
    def _():
        _peer_route(xn_ref, wq_ref, kb_ref, xnt_ref, s_ref, top_ref, cand_ref, kth_ref,
                    r0_ref, e0_ref, cnt_ref, e1_ref, nk=nk, ph=ph, tokens=tokens)
        acc_ref[...] = jnp.zeros_like(acc_ref)

    xu = jnp.dot(u_ref[...], xnt_ref[...], preferred_element_type=F32)
    _peer_gate_chunk(e, xu, coef_ref, r0_ref, e0_ref, cnt_ref, e1_ref,
                     nk=nk, ph=ph, chunk=chunk, tokens=tokens)
    acc_ref[...] += jnp.dot(vt_ref[...], coef_ref[...], preferred_element_type=F32)

    @pl.when(e == pl.num_programs(1) - 1)
    def _():
        y_ref[...] = x1_ref[...] + acc_ref[...].T


def _peer(xn, x1, wq_t, kbig, u_bf, vt_bf):
    N, D = xn.shape
    E = u_bf.shape[0]
    ph = PEER_HEADS
    nk = kbig.shape[1] // ph
    T = min(N, PEER_TOKENS)
    chunk = min(E, PEER_EXPERT_CHUNK)
    assert N % T == 0 and E % chunk == 0 and chunk % nk == 0 and T % LANES == 0
    npairs = len(_candidate_pairs())
    lt = T // LANES
    once = dict(pipeline_mode=pl.Buffered(1))
    return pl.pallas_call(
        functools.partial(_peer_kernel, nk=nk, ph=ph, chunk=chunk, tokens=T),
        grid=(N // T, E // chunk),
        in_specs=[pl.BlockSpec((T, D), lambda t, e: (t, 0)),
                  pl.BlockSpec((T, D), lambda t, e: (t, 0), **once),
                  pl.BlockSpec(wq_t.shape, lambda t, e: (0, 0, 0), **once),
                  pl.BlockSpec(kbig.shape, lambda t, e: (0, 0, 0), **once),
                  pl.BlockSpec((chunk, D), lambda t, e: (e, 0)),
                  pl.BlockSpec((D, chunk), lambda t, e: (0, e))],
        out_specs=pl.BlockSpec((T, D), lambda t, e: (t, 0)),
        out_shape=jax.ShapeDtypeStruct((N, D), F32),
        scratch_shapes=[pltpu.VMEM((2, lt, nk * ph, LANES), F32),
                        pltpu.VMEM((2, PEER_TOPK, lt, ph, LANES), F32),
                        pltpu.VMEM((npairs, lt, ph, LANES), F32),
                        pltpu.VMEM((lt, ph, LANES), F32),
                        pltpu.VMEM((lt, nk * ph, LANES), jnp.uint32),
                        pltpu.VMEM((lt, nk * ph, LANES), jnp.uint32),
                        pltpu.VMEM((lt, ph * nk, LANES), BF16),
                        pltpu.VMEM((lt, ph * nk, LANES), BF16),
                        pltpu.VMEM((D, T), F32),
                        pltpu.VMEM((chunk, T), BF16),
                        pltpu.VMEM((D, T), BF16)],
        compiler_params=_params(("parallel", "arbitrary")),
        name="peer",
    )(xn, x1, wq_t, kbig, u_bf, vt_bf)


def _layer_weights(norm1_g, w_in, q_norm_g, k_norm_g, conv_w, conv_b, conv_ln_g, conv_ln_b,
                   w_out, norm2_g, peer_wq, peer_keys, peer_u, peer_v):
    att_w = (w_in.shape[1] - 2 * conv_w.shape[1]) // 3
    heads = att_w // HEAD_DIM
    row = lambda v: v.reshape(1, -1)
    head_of = jnp.arange(att_w) // HEAD_DIM
    gm = jnp.where(head_of[:, None] == head_of[None, :], 1.0 / HEAD_DIM, 0.0).astype(BF16)
    _, nk, half = peer_keys.shape
    ph = PEER_HEADS
    D = peer_wq.shape[0]
    wq_t = peer_wq.reshape(D, ph, 2, half).transpose(2, 1, 3, 0).reshape(2, ph * half, D)
    eye = jnp.eye(ph, dtype=peer_keys.dtype)
    kbig = (peer_keys[:, :, None, None, :] * eye[None, None, :, :, None]
            ).reshape(2, nk * ph, ph * half)
    return dict(
        g1=row(norm1_g), w_in=w_in.astype(BF16),
        gq=row(jnp.tile(q_norm_g, heads)), gk=row(jnp.tile(k_norm_g, heads)), gm=gm,
        cw=conv_w, cb=row(conv_b), lg=row(conv_ln_g), lb=row(conv_ln_b),
        wa=w_out[:att_w].astype(BF16), wc=w_out[att_w:].astype(BF16), g2=row(norm2_g),
        wq_t=wq_t.astype(BF16), kbig=kbig.astype(BF16),
        u=peer_u.astype(BF16), vt=peer_v.T.astype(BF16))


def _layer(x, caches, conv_hist, w, layer):
    B, S, D = x.shape
    weights = (w["g1"], w["w_in"], w["gq"], w["gk"], w["gm"])
    if caches is None:
        q, kt, vt, kt_bf, v_bf, glu = _inproj(x, *weights, transposed=True)
        att = _attention(q, kt_bf, v_bf)
        k, v = jnp.swapaxes(kt, -1, -2), jnp.swapaxes(vt, -1, -2)
        conv_hist = jnp.zeros((B, w["cw"].shape[0] - 1, glu.shape[-1]), x.dtype)
    else:
        q, k, v, glu = _inproj(x, *weights, transposed=False)
        kt_new = jnp.swapaxes(k, -1, -2).astype(BF16)
        att = _attention(q, kt_new, v.astype(BF16), *caches, layer=layer)
    c, new_conv = _conv(glu, conv_hist, w["cw"], w["cb"], w["lg"], w["lb"])
    x1, xn = _outproj(att.reshape(B * S, -1), c.reshape(B * S, -1), x.reshape(B * S, D),
                      w["wa"], w["wc"], w["g2"])
    y = _peer(xn, x1, w["wq_t"], w["kbig"], w["u"], w["vt"])
    return y.reshape(B, S, D), k, v, new_conv


def kernel(x_prompt, x_sample, cache_k, cache_v, state_conv, norm1_g, w_in, q_norm_g, k_norm_g,
           conv_w, conv_b, conv_ln_g, conv_ln_b, w_out, norm2_g, peer_wq, peer_keys, peer_u,
           peer_v):
    y_p, y_s = x_prompt, x_sample
    cache_k = jnp.swapaxes(cache_k, -1, -2)
    cache_v = jnp.swapaxes(cache_v, -1, -2)
    outs = [[] for _ in range(6)]
    for l in range(norm1_g.shape[0]):
        w = _layer_weights(norm1_g[l], w_in[l], q_norm_g[l], k_norm_g[l], conv_w[l], conv_b[l],
                           conv_ln_g[l], conv_ln_b[l], w_out[l], norm2_g[l], peer_wq[l],
                           peer_keys[l], peer_u[l], peer_v[l])
        y_p, kp, vp, cp = _layer(y_p, None, None, w, l)
        y_s, ks, vs, cs = _layer(y_s, (cache_k, cache_v), state_conv[l], w, l)
        for lst, val in zip(outs, (kp, vp, cp, ks, vs, cs)):
            lst.append(val)
    return (y_p, y_s) + tuple(jnp.stack(o) if len(o) > 1 else o[0][None] for o in outs)
```

```python
import functools
import math

import jax
import jax.numpy as jnp
from jax import lax
from jax.experimental import pallas as pl
from jax.experimental.pallas import tpu as pltpu

F32 = jnp.float32
BF16 = jnp.bfloat16

EPS = 1e-6
HEAD_DIM = 64
PEER_HEADS = 8
PEER_TOPK = 16
LANES = 128
SUBLANES = 8
ROW_TILE = 1024
CONV_ROWS = 64
ATT_BLOCK = 128
PEER_TOKENS = 512
PEER_EXPERT_CHUNK = 2048
PEER_SUB = 16
VMEM_LIMIT = 60 * 1024 * 1024
DEAD_LOG = -120.0

_NT = (((1,), (1,)), ((), ()))


def _const_spec(shape):
    zeros = (0,) * len(shape)
    return pl.BlockSpec(shape, lambda *_: zeros)


def _params(semantics):
    return pltpu.CompilerParams(dimension_semantics=semantics, vmem_limit_bytes=VMEM_LIMIT)


def _inproj_kernel(x_ref, g1_ref, w_ref, gq_ref, gk_ref, gm_ref, q_ref, *out_refs,
                   heads, att_w, conv_ch, transposed):
    x = x_ref[0]
    ms = jnp.mean(x * x, axis=-1, keepdims=True)
    hb = ((x * lax.rsqrt(ms + EPS)) * g1_ref[...]).astype(BF16)

    def proj(c0, width):
        return jnp.dot(hb, w_ref[:, c0:c0 + width], preferred_element_type=F32)

    def head_norm(t, g):
        msq = jnp.dot((t * t).astype(BF16), gm_ref[...], preferred_element_type=F32)
        return (t * lax.rsqrt(msq + EPS)) * g

    qn = head_norm(proj(0, att_w), gq_ref[...]) * (HEAD_DIM ** -0.5)
    kn = head_norm(proj(att_w, att_w), gk_ref[...])
    v = proj(2 * att_w, att_w)
    if transposed:
        kt_ref, vt_ref, ktb_ref, vb_ref, glu_ref = out_refs
        knt, vt = kn.T, v.T
    else:
        k_ref, v_ref, glu_ref = out_refs
    for h in range(heads):
        sl = slice(h * HEAD_DIM, (h + 1) * HEAD_DIM)
        q_ref[0, h] = qn[:, sl].astype(BF16)
        if transposed:
            kt_ref[0, h] = knt[sl, :]
            vt_ref[0, h] = vt[sl, :]
            ktb_ref[0, h] = knt[sl, :].astype(BF16)
            vb_ref[0, h] = v[:, sl].astype(BF16)
        else:
            k_ref[0, h] = kn[:, sl]
            v_ref[0, h] = v[:, sl]
    a = proj(3 * att_w, conv_ch)
    gate = proj(3 * att_w + conv_ch, conv_ch)
    glu_ref[0] = a * jax.nn.sigmoid(gate)


def _inproj(x, g1, w_bf, gq, gk, gm, transposed):
    B, S, D = x.shape
    att_w = gq.shape[-1]
    heads = att_w // HEAD_DIM
    conv_ch = (w_bf.shape[1] - 3 * att_w) // 2
    T = min(S, ROW_TILE)
    assert S % T == 0
    hm = lambda dt: jax.ShapeDtypeStruct((B, heads, S, HEAD_DIM), dt)
    hspec = pl.BlockSpec((1, heads, T, HEAD_DIM), lambda b, s: (b, 0, s, 0))
    tm = lambda dt: jax.ShapeDtypeStruct((B, heads, HEAD_DIM, S), dt)
    tspec = pl.BlockSpec((1, heads, HEAD_DIM, T), lambda b, s: (b, 0, 0, s))
    if transposed:
        kv_specs, kv_shapes = [tspec, tspec, tspec, hspec], [tm(F32), tm(F32), tm(BF16), hm(BF16)]
    else:
        kv_specs, kv_shapes = [hspec, hspec], [hm(F32), hm(F32)]
    return pl.pallas_call(
        functools.partial(_inproj_kernel, heads=heads, att_w=att_w, conv_ch=conv_ch,
                          transposed=transposed),
        grid=(B, S // T),
        in_specs=[pl.BlockSpec((1, T, D), lambda b, s: (b, s, 0)),
                  _const_spec(g1.shape), _const_spec(w_bf.shape),
                  _const_spec(gq.shape), _const_spec(gk.shape), _const_spec(gm.shape)],
        out_specs=[hspec] + kv_specs + [pl.BlockSpec((1, T, conv_ch), lambda b, s: (b, s, 0))],
        out_shape=[hm(BF16)] + kv_shapes + [jax.ShapeDtypeStruct((B, S, conv_ch), F32)],
        compiler_params=_params(("parallel", "parallel")),
        name="inproj",
    )(x, g1, w_bf, gq, gk, gm)


def _conv_kernel(glu_ref, hist_ref, w_ref, b_ref, lg_ref, lb_ref, c_ref, new_ref, ext_ref,
                 *, rows, width, chunk):
    hist = width - 1
    base = 32
    off = base - hist
    s = pl.program_id(1)

    @pl.when(s == 0)
    def _():
        ext_ref[off:base, :] = hist_ref[0]
        ext_ref[base + rows:, :] = jnp.zeros((SUBLANES, ext_ref.shape[1]), F32)

    ext_ref[base:base + rows, :] = glu_ref[0]
    span = chunk + SUBLANES
    lanes_all = ext_ref.shape[1]
    for r0 in range(0, rows, chunk):
        cols = []
        for l0 in range(0, lanes_all, LANES):
            acc = None
            for b in range(SUBLANES):
                part = None
                for a in range((off + width - 1) // SUBLANES + 1):
                    w = SUBLANES * a + b - off
                    if 0 <= w < width:
                        term = (ext_ref[r0 + SUBLANES * a:r0 + SUBLANES * a + span, l0:l0 + LANES]
                                * w_ref[w:w + 1, l0:l0 + LANES])
                        part = term if part is None else part + term
                if part is not None:
                    acc = part[b:b + chunk] if acc is None else acc + part[b:b + chunk]
            cols.append(acc)
        acc = jnp.concatenate(cols, axis=1) + b_ref[...]
        mu = jnp.mean(acc, axis=-1, keepdims=True)
        d = acc - mu
        var = jnp.mean(d * d, axis=-1, keepdims=True)
        y = d * lax.rsqrt(var + EPS) * lg_ref[...] + lb_ref[...]
        c_ref[0, r0:r0 + chunk, :] = (y * jax.nn.sigmoid(y)).astype(BF16)
    tail = ext_ref[rows + off:rows + base, :]
    new_ref[0] = tail
    ext_ref[off:base, :] = tail


def _conv(glu, hist, cw, cb, lg, lb):
    B, S, C = glu.shape
    width = cw.shape[0]
    T = min(S, ROW_TILE)
    assert S % T == 0 and (T >= width - 1 or S == T)
    chunk = min(T, CONV_ROWS)
    return pl.pallas_call(
        functools.partial(_conv_kernel, rows=T, width=width, chunk=chunk),
        grid=(B, S // T),
        in_specs=[pl.BlockSpec((1, T, C), lambda b, s: (b, s, 0)),
                  pl.BlockSpec((1, width - 1, C), lambda b, s: (b, 0, 0)),
                  _const_spec(cw.shape), _const_spec(cb.shape),
                  _const_spec(lg.shape), _const_spec(lb.shape)],
        out_specs=[pl.BlockSpec((1, T, C), lambda b, s: (b, s, 0)),
                   pl.BlockSpec((1, width - 1, C), lambda b, s: (b, 0, 0))],
        out_shape=[jax.ShapeDtypeStruct((B, S, C), BF16),
                   jax.ShapeDtypeStruct((B, width - 1, C), F32)],
        scratch_shapes=[pltpu.VMEM((32 + T + SUBLANES, C), F32)],
        compiler_params=_params(("parallel", "arbitrary")),
        name="conv",
    )(glu, hist, cw, cb, lg, lb)


def _sb_block(qs, kts, vbs, carries, upper, ones, mask, v_transposed):
    tq, tk = qs[0].shape[0], kts[0].shape[1]
    zs = [jnp.dot(q, kt, preferred_element_type=F32) for q, kt in zip(qs, kts)]
    log_betas, terms = [], []
    for z in zs:
        sp = jnp.log1p(jnp.exp(-jnp.abs(z)))
        log_beta = jnp.minimum(z, 0.0) - sp
        log_1m = log_beta - z
        if mask is not None:
            log_1m = jnp.where(mask, log_1m, 0.0)
        hi = log_1m.astype(BF16)
        lo = (log_1m - hi.astype(F32)).astype(BF16)
        log_betas.append(log_beta)
        terms += [hi, lo]
    both = jnp.concatenate(terms, axis=0)
    if tk == LANES:
        p = jnp.dot(both, jnp.concatenate([upper, ones], axis=1), preferred_element_type=F32)
        after, total = p[:, :tk], p[:, tk:]
    else:
        after = jnp.dot(both, upper, preferred_element_type=F32)
        total = jnp.dot(both, ones, preferred_element_type=F32)
    sums = []
    for h in range(len(zs)):
        r = 2 * h * tq
        sums.append((after[r:r + tq] + after[r + tq:r + 2 * tq],
                     total[r:r + tq] + total[r + tq:r + 2 * tq]))
    outs, new_carries = [], []
    for log_beta, (after, total), carry, vb in zip(log_betas, sums, carries, vbs):
        logw = log_beta + after
        if carry is not None:
            logw = logw + (carry if tk == LANES else carry[:, :1])
        w = jnp.exp(logw)
        if mask is not None:
            w = jnp.where(mask, w, 0.0)
        if v_transposed:
            outs.append(lax.dot_general(w.astype(BF16), vb, _NT, preferred_element_type=F32))
        else:
            outs.append(jnp.dot(w.astype(BF16), vb, preferred_element_type=F32))
        new_carries.append(total if carry is None else carry + total)
    return outs, new_carries


def _max_all(xs):
    m = xs[0]
    for x in xs[1:]:
        m = jnp.maximum(m, x)
    return jnp.max(m)


def _attn_kernel(*refs, heads, tq, tk, old_blocks, diag_in_old):
    if diag_in_old:
        q_ref, ko_ref, vo_ref, o_ref, acc_ref, carry_ref = refs
    else:
        q_ref, kd_ref, vd_ref, ko_ref, vo_ref, o_ref, acc_ref, carry_ref = refs
    qi = pl.program_id(1)

    def tri(n):
        r = lax.broadcasted_iota(jnp.int32, (n, n), 0)
        c = lax.broadcasted_iota(jnp.int32, (n, n), 1)
        return r, c

    r, c = tri(tq)
    mask = c < r
    upper_d = jnp.where(r > c, 1.0, 0.0).astype(BF16)
    ones_d = jnp.ones((tq, LANES), BF16)
    hs = range(heads)
    qs = [q_ref[0, h] for h in hs]
    if diag_in_old:
        rows = pl.ds(pl.multiple_of(qi * tq, tq), tq)
        kts = [ko_ref[0, h, :, rows] for h in hs]
        vbs = [vo_ref[0, h, rows, :] for h in hs]
    else:
        kts = [kd_ref[0, h] for h in hs]
        vbs = [vd_ref[0, h] for h in hs]
    outs, carries = _sb_block(qs, kts, vbs, [None] * heads, upper_d, ones_d, mask, False)
    for h in hs:
        acc_ref[h] = outs[h]
        carry_ref[h] = carries[h]

    if old_blocks is None:
        j0 = qi - 1
    else:
        j0 = jnp.int32(old_blocks - 1)

    r, c = tri(tk)
    upper_o = jnp.where(r > c, 1.0, 0.0).astype(BF16)
    ones_o = jnp.ones((tk, LANES), BF16)

    def cond(state):
        j, live = state
        return jnp.logical_and(j >= 0, live > DEAD_LOG)

    def body(state):
        j, _ = state
        rows = pl.ds(pl.multiple_of(j * tk, tk), tk)
        qs = [q_ref[0, h] for h in hs]
        kts = [ko_ref[0, h, :, rows].astype(BF16) for h in hs]
        if diag_in_old:
            vbs = [vo_ref[0, h, rows, :] for h in hs]
        else:
            vbs = [vo_ref[0, h, :, rows].astype(BF16) for h in hs]
        old = [carry_ref[h] for h in hs]
        accs = [acc_ref[h] for h in hs]
        outs, new = _sb_block(qs, kts, vbs, old, upper_o, ones_o, None, not diag_in_old)
        for h in hs:
            acc_ref[h] = accs[h] + outs[h]
            carry_ref[h] = new[h]
        return j - 1, _max_all(new)

    lax.while_loop(cond, body, (j0, _max_all(carries)))
    for h in range(heads):
        o_ref[0, :, h * HEAD_DIM:(h + 1) * HEAD_DIM] = acc_ref[h].astype(BF16)


def _attention(q, kt_new, v_new, kt_old=None, vt_old=None, layer=0):
    B, H, S, d = q.shape
    if kt_old is None:
        tq = tk = min(S, ATT_BLOCK)
        assert S % tq == 0
        old_blocks, diag_in_old = None, True
        in_specs = [pl.BlockSpec((1, H, tq, d), lambda b, i: (b, 0, i, 0)),
                    pl.BlockSpec((1, H, d, S), lambda b, i: (b, 0, 0, 0)),
                    pl.BlockSpec((1, H, S, d), lambda b, i: (b, 0, 0, 0))]
        args = (q, kt_new, v_new)
    else:
        P = kt_old.shape[4]
        tq, tk = S, min(P, ATT_BLOCK)
        assert P % tk == 0
        old_blocks, diag_in_old = P // tk, False
        new = pl.BlockSpec((1, H, S, d), lambda b, i: (b, 0, 0, 0))
        new_t = pl.BlockSpec((1, H, d, S), lambda b, i: (b, 0, 0, 0))
        old = pl.BlockSpec((None, 1, H, d, P), lambda b, i: (layer, b, 0, 0, 0))
        in_specs = [new, new_t, new, old, old]
        args = (q, kt_new, v_new, kt_old, vt_old)
    return pl.pallas_call(
        functools.partial(_attn_kernel, heads=H, tq=tq, tk=tk, old_blocks=old_blocks,
                          diag_in_old=diag_in_old),
        grid=(B, S // tq),
        in_specs=in_specs,
        out_specs=pl.BlockSpec((1, tq, H * d), lambda b, i: (b, i, 0)),
        out_shape=jax.ShapeDtypeStruct((B, S, H * d), BF16),
        scratch_shapes=[pltpu.VMEM((H, tq, d), F32), pltpu.VMEM((H, tq, LANES), F32)],
        compiler_params=_params(("parallel", "arbitrary")),
        name="attention",
    )(*args)


def _outproj_kernel(att_ref, c_ref, x_ref, wa_ref, wc_ref, g2_ref, x1_ref, xn_ref):
    x1 = (x_ref[...]
          + jnp.dot(att_ref[...], wa_ref[...], preferred_element_type=F32)
          + jnp.dot(c_ref[...], wc_ref[...], preferred_element_type=F32))
    x1_ref[...] = x1
    ms = jnp.mean(x1 * x1, axis=-1, keepdims=True)
    xn_ref[...] = ((x1 * lax.rsqrt(ms + EPS)) * g2_ref[...]).astype(BF16)


def _outproj(att, c, x, wa, wc, g2):
    N, D = x.shape
    T = min(N, ROW_TILE)
    assert N % T == 0
    row = lambda w: pl.BlockSpec((T, w), lambda i: (i, 0))
    return pl.pallas_call(
        _outproj_kernel,
        grid=(N // T,),
        in_specs=[row(att.shape[1]), row(c.shape[1]), row(D),
                  _const_spec(wa.shape), _const_spec(wc.shape), _const_spec(g2.shape)],
        out_specs=[row(D), row(D)],
        out_shape=[jax.ShapeDtypeStruct((N, D), F32), jax.ShapeDtypeStruct((N, D), BF16)],
        compiler_params=_params(("parallel",)),
        name="outproj",
    )(att, c, x, wa, wc, g2)


def _candidate_pairs():
    return [(p, q) for p in range(1, PEER_TOPK + 1) for q in range(1, PEER_TOPK + 1)
            if p * q <= PEER_TOPK]


def _sort_desc(v):
    v = list(v)
    n = len(v)
    k = 2
    while k <= n:
        j = k // 2
        while j >= 1:
            for i in range(n):
                m = i ^ j
                if m > i:
                    hi, lo = jnp.maximum(v[i], v[m]), jnp.minimum(v[i], v[m])
                    v[i], v[m] = (hi, lo) if (i & k) == 0 else (lo, hi)
            j //= 2
        k *= 2
    return v


def _merge_top(a, b):
    n = len(a)
    c = [jnp.maximum(a[i], b[n - 1 - i]) for i in range(n)]
    j = n // 2
    while j >= 1:
        for i in range(n):
            m = i ^ j
            if m > i:
                c[i], c[m] = jnp.maximum(c[i], c[m]), jnp.minimum(c[i], c[m])
        j //= 2
    return c


def _top_values(load, count):
    k = PEER_TOPK
    groups = []
    for g in range(0, count, k):
        vals = [load(n) for n in range(g, min(g + k, count))]
        vals += [jnp.full_like(vals[0], -jnp.inf)] * (k - len(vals))
        groups.append(_sort_desc(vals))
    while len(groups) > 1:
        merged = [_merge_top(groups[i], groups[i + 1]) for i in range(0, len(groups) - 1, 2)]
        groups = merged + ([groups[-1]] if len(groups) % 2 else [])
    return groups[0]


def _peer_route(xn_ref, wq_ref, kb_ref, xnt_ref, s_ref, top_ref, cand_ref, kth_ref,
                r0_ref, e0_ref, cnt_ref, e1_ref, *, nk, ph, tokens):
    xnt_ref[...] = xn_ref[...].astype(F32).T.astype(BF16)
    xnt = xnt_ref[...]
    for c in range(2):
        qt = jnp.dot(wq_ref[c], xnt, preferred_element_type=F32)
        sc = jnp.dot(kb_ref[c], qt.astype(BF16), preferred_element_type=F32)
        for lt in range(tokens // LANES):
            s_ref[c, lt] = sc[:, lt * LANES:(lt + 1) * LANES]

    pairs = _candidate_pairs()
    for lt in range(tokens // LANES):
        for c in range(2):
            top = _top_values(lambda n: s_ref[c, lt, n * ph:(n + 1) * ph, :], nk)
            for p in range(PEER_TOPK):
                top_ref[c, p, lt] = top[p]
        for n, (p, q) in enumerate(pairs):
            cand_ref[n, lt] = top_ref[0, p - 1, lt] + top_ref[1, q - 1, lt]
        kth_ref[lt] = _top_values(lambda n: cand_ref[n, lt], len(pairs))[PEER_TOPK - 1]
    theta = kth_ref[...]
    a1 = top_ref[0, 0]
    b1 = top_ref[1, 0]
    best = a1 + b1
    denom = None
    for n in range(len(pairs)):
        cv = cand_ref[n]
        t = jnp.where(cv >= theta, jnp.exp(cv - best), 0.0)
        denom = t if denom is None else denom + t
    inv = 1.0 / denom

    def first_half(i, _):
        rows = pl.ds(pl.multiple_of(i * ph, ph), ph)
        s0 = s_ref[0, :, rows, :]
        rank = jnp.ones_like(s0)
        for p in range(PEER_TOPK):
            rank = jnp.where(top_ref[0, p] > s0, p + 2.0, rank)
        r0_ref[:, rows, :] = _bf16_pair(rank)
        e0_ref[:, rows, :] = _bf16_pair(jnp.exp(s0 - a1) * inv)
        return 0
    lax.fori_loop(0, nk, first_half, 0)

    def second_half(j, _):
        rows = pl.ds(pl.multiple_of(j * ph, ph), ph)
        s1 = s_ref[1, :, rows, :]
        cnt = jnp.zeros_like(s1)
        for p in range(PEER_TOPK):
            cnt = jnp.where(top_ref[0, p] + s1 >= theta, p + 1.0, cnt)
        s_ref[1, :, rows, :] = cnt
        s_ref[0, :, rows, :] = jnp.exp(s1 - b1)
        return 0
    lax.fori_loop(0, nk, second_half, 0)
    for lt in range(tokens // LANES):
        for h in range(ph):
            cnt_ref[lt, h * nk:(h + 1) * nk, :] = (
                s_ref[1, lt, pl.ds(h, nk, stride=ph), :].astype(BF16))
            e1_ref[lt, h * nk:(h + 1) * nk, :] = (
                s_ref[0, lt, pl.ds(h, nk, stride=ph), :].astype(BF16))


def _bf16_pair(x):
    bits = pltpu.bitcast(x.astype(BF16).astype(F32), jnp.uint32)
    return bits | lax.shift_right_logical(bits, jnp.uint32(16))


def _zero_bits_after(x):
    bits = pltpu.bitcast(x, jnp.uint32)
    return lax.shift_right_logical(lax.shift_right_logical(bits, jnp.uint32(16)), jnp.uint32(16))


def _peer_gate_chunk(c, xu, coef_ref, r0_ref, e0_ref, cnt_ref, e1_ref, *, nk, ph, chunk, tokens):
    half_tile = (PEER_SUB // 2, LANES)

    for ib in range(chunk // nk):
        i = c * (chunk // nk) + ib
        hrows = pl.ds(pl.multiple_of(i * ph, ph), ph)
        row0 = ib * nk
        for lt in range(tokens // LANES):
            lanes = slice(lt * LANES, (lt + 1) * LANES)
            zero = _zero_bits_after(xu[row0:row0 + SUBLANES, lanes])
            rank = r0_ref[lt, hrows, :] | zero
            e0 = e0_ref[lt, hrows, :] | zero
            w = [None] * (nk // PEER_SUB)
            for h in range(ph):
                if h:
                    zero = _zero_bits_after(pltpu.bitcast(w[-1], jnp.uint32))
                rank_b = pltpu.bitcast(jnp.broadcast_to(rank[h:h + 1], half_tile) | zero, BF16)
                e0_b = pltpu.bitcast(jnp.broadcast_to(e0[h:h + 1], half_tile) | zero, BF16)
                for jb in range(nk // PEER_SUB):
                    jrows = slice(h * nk + jb * PEER_SUB, h * nk + (jb + 1) * PEER_SUB)
                    t = jnp.where(rank_b <= cnt_ref[lt, jrows, :],
                                  e1_ref[lt, jrows, :], 0.0) * e0_b
                    w[jb] = t if w[jb] is None else w[jb] + t
            for jb in range(nk // PEER_SUB):
                erows = slice(row0 + jb * PEER_SUB, row0 + (jb + 1) * PEER_SUB)
                xs = xu[erows, lanes].astype(BF16)
                act = xs * (0.5 + 0.5 * lax.erf(xs * math.sqrt(0.5)))
                coef_ref[erows, lanes] = w[jb] * act


def _peer_kernel(xn_ref, x1_ref, wq_ref, kb_ref, u_ref, vt_ref, y_ref,
                 s_ref, top_ref, cand_ref, kth_ref, r0_ref, e0_ref, cnt_ref, e1_ref,
                 acc_ref, coef_ref, xnt_ref, *, nk, ph, chunk, tokens):
    e = pl.program_id(1)

    @pl.when(e == 0)
    def _():
        _peer_route(xn_ref, wq_ref, kb_ref, xnt_ref, s_ref, top_ref, cand_ref, kth_ref,
                    r0_ref, e0_ref, cnt_ref, e1_ref, nk=nk, ph=ph, tokens=tokens)
        acc_ref[...] = jnp.zeros_like(acc_ref)

    xu = jnp.dot(u_ref[...], xnt_ref[...], preferred_element_type=F32)
    _peer_gate_chunk(e, xu, coef_ref, r0_ref, e0_ref, cnt_ref, e1_ref,
                     nk=nk, ph=ph, chunk=chunk, tokens=tokens)
    acc_ref[...] += jnp.dot(vt_ref[...], coef_ref[...], preferred_element_type=F32)

    @pl.when(e == pl.num_programs(1) - 1)
    def _():
        y_ref[...] = x1_ref[...] + acc_ref[...].T


def _peer(xn, x1, wq_t, kbig, u_bf, vt_bf):
    N, D = xn.shape
    E = u_bf.shape[0]
    ph = PEER_HEADS
    nk = kbig.shape[1] // ph
    T = min(N, PEER_TOKENS)
    chunk = min(E, PEER_EXPERT_CHUNK)
    assert N % T == 0 and E % chunk == 0 and chunk % nk == 0 and T % LANES == 0
    npairs = len(_candidate_pairs())
    lt = T // LANES
    once = dict(pipeline_mode=pl.Buffered(1))
    return pl.pallas_call(
        functools.partial(_peer_kernel, nk=nk, ph=ph, chunk=chunk, tokens=T),
        grid=(N // T, E // chunk),
        in_specs=[pl.BlockSpec((T, D), lambda t, e: (t, 0)),
                  pl.BlockSpec((T, D), lambda t, e: (t, 0), **once),
                  pl.BlockSpec(wq_t.shape, lambda t, e: (0, 0, 0), **once),
                  pl.BlockSpec(kbig.shape, lambda t, e: (0, 0, 0), **once),
                  pl.BlockSpec((chunk, D), lambda t, e: (e, 0)),
                  pl.BlockSpec((D, chunk), lambda t, e: (0, e))],
        out_specs=pl.BlockSpec((T, D), lambda t, e: (t, 0)),
        out_shape=jax.ShapeDtypeStruct((N, D), F32),
        scratch_shapes=[pltpu.VMEM((2, lt, nk * ph, LANES), F32),
                        pltpu.VMEM((2, PEER_TOPK, lt, ph, LANES), F32),
                        pltpu.VMEM((npairs, lt, ph, LANES), F32),
                        pltpu.VMEM((lt, ph, LANES), F32),
                        pltpu.VMEM((lt, nk * ph, LANES), jnp.uint32),
                        pltpu.VMEM((lt, nk * ph, LANES), jnp.uint32),
                        pltpu.VMEM((lt, ph * nk, LANES), BF16),
                        pltpu.VMEM((lt, ph * nk, LANES), BF16),
                        pltpu.VMEM((D, T), F32),
                        pltpu.VMEM((chunk, T), BF16),
                        pltpu.VMEM((D, T), BF16)],
        compiler_params=_params(("parallel", "arbitrary")),
        name="peer",
    )(xn, x1, wq_t, kbig, u_bf, vt_bf)


def _layer_weights(norm1_g, w_in, q_norm_g, k_norm_g, conv_w, conv_b, conv_ln_g, conv_ln_b,
                   w_out, norm2_g, peer_wq, peer_keys, peer_u, peer_v):
    att_w = (w_in.shape[1] - 2 * conv_w.shape[1]) // 3
    heads = att_w // HEAD_DIM
    row = lambda v: v.reshape(1, -1)
    head_of = jnp.arange(att_w) // HEAD_DIM
    gm = jnp.where(head_of[:, None] == head_of[None, :], 1.0 / HEAD_DIM, 0.0).astype(BF16)
    _, nk, half = peer_keys.shape
    ph = PEER_HEADS
    D = peer_wq.shape[0]
    wq_t = peer_wq.reshape(D, ph, 2, half).transpose(2, 1, 3, 0).reshape(2, ph * half, D)
    eye = jnp.eye(ph, dtype=peer_keys.dtype)
    kbig = (peer_keys[:, :, None, None, :] * eye[None, None, :, :, None]
            ).reshape(2, nk * ph, ph * half)
    return dict(
        g1=row(norm1_g), w_in=w_in.astype(BF16),
        gq=row(jnp.tile(q_norm_g, heads)), gk=row(jnp.tile(k_norm_g, heads)), gm=gm,
        cw=conv_w, cb=row(conv_b), lg=row(conv_ln_g), lb=row(conv_ln_b),
        wa=w_out[:att_w].astype(BF16), wc=w_out[att_w:].astype(BF16), g2=row(norm2_g),
        wq_t=wq_t.astype(BF16), kbig=kbig.astype(BF16),
        u=peer_u.astype(BF16), vt=peer_v.T.astype(BF16))


def _layer(x, caches, conv_hist, w, layer):
    B, S, D = x.shape
    weights = (w["g1"], w["w_in"], w["gq"], w["gk"], w["gm"])
    if caches is None:
        q, kt, vt, kt_bf, v_bf, glu = _inproj(x, *weights, transposed=True)
        att = _attention(q, kt_bf, v_bf)
        k, v = jnp.swapaxes(kt, -1, -2), jnp.swapaxes(vt, -1, -2)
        conv_hist = jnp.zeros((B, w["cw"].shape[0] - 1, glu.shape[-1]), x.dtype)
    else:
        q, k, v, glu = _inproj(x, *weights, transposed=False)
        kt_new = jnp.swapaxes(k, -1, -2).astype(BF16)
        att = _attention(q, kt_new, v.astype(BF16), *caches, layer=layer)
    c, new_conv = _conv(glu, conv_hist, w["cw"], w["cb"], w["lg"], w["lb"])
    x1, xn = _outproj(att.reshape(B * S, -1), c.reshape(B * S, -1), x.reshape(B * S, D),
                      w["wa"], w["wc"], w["g2"])
    y = _peer(xn, x1, w["wq_t"], w["kbig"], w["u"], w["vt"])
    return y.reshape(B, S, D), k, v, new_conv


def kernel(x_prompt, x_sample, cache_k, cache_v, state_conv, norm1_g, w_in, q_norm_g, k_norm_g,
           conv_w, conv_b, conv_ln_g, conv_ln_b, w_out, norm2_g, peer_wq, peer_keys, peer_u,
           peer_v):
    y_p, y_s = x_prompt, x_sample
    cache_k = jnp.swapaxes(cache_k, -1, -2)
    cache_v = jnp.swapaxes(cache_v, -1, -2)
    outs = [[] for _ in range(6)]
    for l in range(norm1_g.shape[0]):
        w = _layer_weights(norm1_g[l], w_in[l], q_norm_g[l], k_norm_g[l], conv_w[l], conv_b[l],
                           conv_ln_g[l], conv_ln_b[l], w_out[l], norm2_g[l], peer_wq[l],
                           peer_keys[l], peer_u[l], peer_v[l])
        y_p, kp, vp, cp = _layer(y_p, None, None, w, l)
        y_s, ks, vs, cs = _layer(y_s, (cache_k, cache_v), state_conv[l], w, l)
        for lst, val in zip(outs, (kp, vp, cp, ks, vs, cs)):
            lst.append(val)
    return (y_p, y_s) + tuple(jnp.stack(o) if len(o) > 1 else o[0][None] for o in outs)
```

```python
import functools
import math

import jax
import jax.numpy as jnp
from jax import lax
from jax.experimental import pallas as pl
from jax.experimental.pallas import tpu as pltpu

F32 = jnp.float32
BF16 = jnp.bfloat16

EPS = 1e-6
HEAD_DIM = 64
PEER_HEADS = 8
PEER_TOPK = 16
LANES = 128
SUBLANES = 8
ROW_TILE = 1024
CONV_ROWS = 64
ATT_BLOCK = 128
PEER_TOKENS = 512
PEER_EXPERT_CHUNK = 2048
PEER_SUB = 16
VMEM_LIMIT = 60 * 1024 * 1024
DEAD_LOG = -120.0

_NT = (((1,), (1,)), ((), ()))


def _const_spec(shape):
    zeros = (0,) * len(shape)
    return pl.BlockSpec(shape, lambda *_: zeros)


def _params(semantics):
    return pltpu.CompilerParams(dimension_semantics=semantics, vmem_limit_bytes=VMEM_LIMIT)


def _inproj_kernel(x_ref, g1_ref, w_ref, gq_ref, gk_ref, gm_ref, q_ref, *out_refs,
                   heads, att_w, conv_ch, transposed):
    x = x_ref[0]
    ms = jnp.mean(x * x, axis=-1, keepdims=True)
    hb = ((x * lax.rsqrt(ms + EPS)) * g1_ref[...]).astype(BF16)

    def proj(c0, width):
        return jnp.dot(hb, w_ref[:, c0:c0 + width], preferred_element_type=F32)

    def head_norm(t, g):
        msq = jnp.dot((t * t).astype(BF16), gm_ref[...], preferred_element_type=F32)
        return (t * lax.rsqrt(msq + EPS)) * g

    qn = head_norm(proj(0, att_w), gq_ref[...]) * (HEAD_DIM ** -0.5)
    kn = head_norm(proj(att_w, att_w), gk_ref[...])
    v = proj(2 * att_w, att_w)
    if transposed:
        kt_ref, vt_ref, ktb_ref, vb_ref, glu_ref = out_refs
        knt, vt = kn.T, v.T
    else:
        k_ref, v_ref, glu_ref = out_refs
    for h in range(heads):
        sl = slice(h * HEAD_DIM, (h + 1) * HEAD_DIM)
        q_ref[0, h] = qn[:, sl].astype(BF16)
        if transposed:
            kt_ref[0, h] = knt[sl, :]
            vt_ref[0, h] = vt[sl, :]
            ktb_ref[0, h] = knt[sl, :].astype(BF16)
            vb_ref[0, h] = v[:, sl].astype(BF16)
        else:
            k_ref[0, h] = kn[:, sl]
            v_ref[0, h] = v[:, sl]
    a = proj(3 * att_w, conv_ch)
    gate = proj(3 * att_w + conv_ch, conv_ch)
    glu_ref[0] = a * jax.nn.sigmoid(gate)


def _inproj(x, g1, w_bf, gq, gk, gm, transposed):
    B, S, D = x.shape
    att_w = gq.shape[-1]
    heads = att_w // HEAD_DIM
    conv_ch = (w_bf.shape[1] - 3 * att_w) // 2
    T = min(S, ROW_TILE)
    assert S % T == 0
    hm = lambda dt: jax.ShapeDtypeStruct((B, heads, S, HEAD_DIM), dt)
    hspec = pl.BlockSpec((1, heads, T, HEAD_DIM), lambda b, s: (b, 0, s, 0))
    tm = lambda dt: jax.ShapeDtypeStruct((B, heads, HEAD_DIM, S), dt)
    tspec = pl.BlockSpec((1, heads, HEAD_DIM, T), lambda b, s: (b, 0, 0, s))
    if transposed:
        kv_specs, kv_shapes = [tspec, tspec, tspec, hspec], [tm(F32), tm(F32), tm(BF16), hm(BF16)]
    else:
        kv_specs, kv_shapes = [hspec, hspec], [hm(F32), hm(F32)]
    return pl.pallas_call(
        functools.partial(_inproj_kernel, heads=heads, att_w=att_w, conv_ch=conv_ch,
                          transposed=transposed),
        grid=(B, S // T),
        in_specs=[pl.BlockSpec((1, T, D), lambda b, s: (b, s, 0)),
                  _const_spec(g1.shape), _const_spec(w_bf.shape),
                  _const_spec(gq.shape), _const_spec(gk.shape), _const_spec(gm.shape)],
        out_specs=[hspec] + kv_specs + [pl.BlockSpec((1, T, conv_ch), lambda b, s: (b, s, 0))],
        out_shape=[hm(BF16)] + kv_shapes + [jax.ShapeDtypeStruct((B, S, conv_ch), F32)],
        compiler_params=_params(("parallel", "parallel")),
        name="inproj",
    )(x, g1, w_bf, gq, gk, gm)


def _conv_kernel(glu_ref, hist_ref, w_ref, b_ref, lg_ref, lb_ref, c_ref, new_ref, ext_ref,
                 *, rows, width, chunk):
    hist = width - 1
    base = 32
    off = base - hist
    s = pl.program_id(1)

    @pl.when(s == 0)
    def _():
        ext_ref[off:base, :] = hist_ref[0]
        ext_ref[base + rows:, :] = jnp.zeros((SUBLANES, ext_ref.shape[1]), F32)

    ext_ref[base:base + rows, :] = glu_ref[0]
    span = chunk + SUBLANES
    lanes_all = ext_ref.shape[1]
    for r0 in range(0, rows, chunk):
        cols = []
        for l0 in range(0, lanes_all, LANES):
            acc = None
            for b in range(SUBLANES):
                part = None
                for a in range((off + width - 1) // SUBLANES + 1):
                    w = SUBLANES * a + b - off
                    if 0 <= w < width:
                        term = (ext_ref[r0 + SUBLANES * a:r0 + SUBLANES * a + span, l0:l0 + LANES]
                                * w_ref[w:w + 1, l0:l0 + LANES])
                        part = term if part is None else part + term
                if part is not None:
                    acc = part[b:b + chunk] if acc is None else acc + part[b:b + chunk]
            cols.append(acc)
        acc = jnp.concatenate(cols, axis=1) + b_ref[...]
        mu = jnp.mean(acc, axis=-1, keepdims=True)
        d = acc - mu
        var = jnp.mean(d * d, axis=-1, keepdims=True)
        y = d * lax.rsqrt(var + EPS) * lg_ref[...] + lb_ref[...]
        c_ref[0, r0:r0 + chunk, :] = (y * jax.nn.sigmoid(y)).astype(BF16)
    tail = ext_ref[rows + off:rows + base, :]
    new_ref[0] = tail
    ext_ref[off:base, :] = tail


def _conv(glu, hist, cw, cb, lg, lb):
    B, S, C = glu.shape
    width = cw.shape[0]
    T = min(S, ROW_TILE)
    assert S % T == 0 and (T >= width - 1 or S == T)
    chunk = min(T, CONV_ROWS)
    return pl.pallas_call(
        functools.partial(_conv_kernel, rows=T, width=width, chunk=chunk),
        grid=(B, S // T),
        in_specs=[pl.BlockSpec((1, T, C), lambda b, s: (b, s, 0)),
                  pl.BlockSpec((1, width - 1, C), lambda b, s: (b, 0, 0)),
                  _const_spec(cw.shape), _const_spec(cb.shape),
                  _const_spec(lg.shape), _const_spec(lb.shape)],
        out_specs=[pl.BlockSpec((1, T, C), lambda b, s: (b, s, 0)),
                   pl.BlockSpec((1, width - 1, C), lambda b, s: (b, 0, 0))],
        out_shape=[jax.ShapeDtypeStruct((B, S, C), BF16),
                   jax.ShapeDtypeStruct((B, width - 1, C), F32)],
        scratch_shapes=[pltpu.VMEM((32 + T + SUBLANES, C), F32)],
        compiler_params=_params(("parallel", "arbitrary")),
        name="conv",
    )(glu, hist, cw, cb, lg, lb)


def _sb_block(qs, kts, vbs, carries, upper, ones, mask, v_transposed):
    tq, tk = qs[0].shape[0], kts[0].shape[1]
    zs = [jnp.dot(q, kt, preferred_element_type=F32) for q, kt in zip(qs, kts)]
    log_betas, terms = [], []
    for z in zs:
        sp = jnp.log1p(jnp.exp(-jnp.abs(z)))
        log_beta = jnp.minimum(z, 0.0) - sp
        log_1m = log_beta - z
        if mask is not None:
            log_1m = jnp.where(mask, log_1m, 0.0)
        hi = log_1m.astype(BF16)
        lo = (log_1m - hi.astype(F32)).astype(BF16)
        log_betas.append(log_beta)
        terms += [hi, lo]
    both = jnp.concatenate(terms, axis=0)
    if tk == LANES:
        p = jnp.dot(both, jnp.concatenate([upper, ones], axis=1), preferred_element_type=F32)
        after, total = p[:, :tk], p[:, tk:]
    else:
        after = jnp.dot(both, upper, preferred_element_type=F32)
        total = jnp.dot(both, ones, preferred_element_type=F32)
    sums = []
    for h in range(len(zs)):
        r = 2 * h * tq
        sums.append((after[r:r + tq] + after[r + tq:r + 2 * tq],
                     total[r:r + tq] + total[r + tq:r + 2 * tq]))
    outs, new_carries = [], []
    for log_beta, (after, total), carry, vb in zip(log_betas, sums, carries, vbs):
        logw = log_beta + after
        if carry is not None:
            logw = logw + (carry if tk == LANES else carry[:, :1])
        w = jnp.exp(logw)
        if mask is not None:
            w = jnp.where(mask, w, 0.0)
        if v_transposed:
            outs.append(lax.dot_general(w.astype(BF16), vb, _NT, preferred_element_type=F32))
        else:
            outs.append(jnp.dot(w.astype(BF16), vb, preferred_element_type=F32))
        new_carries.append(total if carry is None else carry + total)
    return outs, new_carries


def _max_all(xs):
    m = xs[0]
    for x in xs[1:]:
        m = jnp.maximum(m, x)
    return jnp.max(m)


def _attn_kernel(*refs, heads, tq, tk, old_blocks, diag_in_old):
    if diag_in_old:
        q_ref, ko_ref, vo_ref, o_ref, acc_ref, carry_ref = refs
    else:
        q_ref, kd_ref, vd_ref, ko_ref, vo_ref, o_ref, acc_ref, carry_ref = refs
    qi = pl.program_id(1)

    def tri(n):
        r = lax.broadcasted_iota(jnp.int32, (n, n), 0)
        c = lax.broadcasted_iota(jnp.int32, (n, n), 1)
        return r, c

    r, c = tri(tq)
    mask = c < r
    upper_d = jnp.where(r > c, 1.0, 0.0).astype(BF16)
    ones_d = jnp.ones((tq, LANES), BF16)
    hs = range(heads)
    qs = [q_ref[0, h] for h in hs]
    if diag_in_old:
        rows = pl.ds(pl.multiple_of(qi * tq, tq), tq)
        kts = [ko_ref[0, h, :, rows] for h in hs]
        vbs = [vo_ref[0, h, rows, :] for h in hs]
    else:
        kts = [kd_ref[0, h] for h in hs]
        vbs = [vd_ref[0, h] for h in hs]
    outs, carries = _sb_block(qs, kts, vbs, [None] * heads, upper_d, ones_d, mask, False)
    for h in hs:
        acc_ref[h] = outs[h]
        carry_ref[h] = carries[h]

    if old_blocks is None:
        j0 = qi - 1
    else:
        j0 = jnp.int32(old_blocks - 1)

    r, c = tri(tk)
    upper_o = jnp.where(r > c, 1.0, 0.0).astype(BF16)
    ones_o = jnp.ones((tk, LANES), BF16)

    def cond(state):
        j, live = state
        return jnp.logical_and(j >= 0, live > DEAD_LOG)

    def body(state):
        j, _ = state
        rows = pl.ds(pl.multiple_of(j * tk, tk), tk)
        qs = [q_ref[0, h] for h in hs]
        kts = [ko_ref[0, h, :, rows].astype(BF16) for h in hs]
        if diag_in_old:
            vbs = [vo_ref[0, h, rows, :] for h in hs]
        else:
            vbs = [vo_ref[0, h, :, rows].astype(BF16) for h in hs]
        old = [carry_ref[h] for h in hs]
        accs = [acc_ref[h] for h in hs]
        outs, new = _sb_block(qs, kts, vbs, old, upper_o, ones_o, None, not diag_in_old)
        for h in hs:
            acc_ref[h] = accs[h] + outs[h]
            carry_ref[h] = new[h]
        return j - 1, _max_all(new)

    lax.while_loop(cond, body, (j0, _max_all(carries)))
    for h in range(heads):
        o_ref[0, :, h * HEAD_DIM:(h + 1) * HEAD_DIM] = acc_ref[h].astype(BF16)


def _attention(q, kt_new, v_new, kt_old=None, vt_old=None, layer=0):
    B, H, S, d = q.shape
    if kt_old is None:
        tq = tk = min(S, ATT_BLOCK)
        assert S % tq == 0
        old_blocks, diag_in_old = None, True
        in_specs = [pl.BlockSpec((1, H, tq, d), lambda b, i: (b, 0, i, 0)),
                    pl.BlockSpec((1, H, d, S), lambda b, i: (b, 0, 0, 0)),
                    pl.BlockSpec((1, H, S, d), lambda b, i: (b, 0, 0, 0))]
        args = (q, kt_new, v_new)
    else:
        P = kt_old.shape[4]
        tq, tk = S, min(P, ATT_BLOCK)
        assert P % tk == 0
        old_blocks, diag_in_old = P // tk, False
        new = pl.BlockSpec((1, H, S, d), lambda b, i: (b, 0, 0, 0))
        new_t = pl.BlockSpec((1, H, d, S), lambda b, i: (b, 0, 0, 0))
        old = pl.BlockSpec((None, 1, H, d, P), lambda b, i: (layer, b, 0, 0, 0))
        in_specs = [new, new_t, new, old, old]
        args = (q, kt_new, v_new, kt_old, vt_old)
    return pl.pallas_call(
        functools.partial(_attn_kernel, heads=H, tq=tq, tk=tk, old_blocks=old_blocks,
                          diag_in_old=diag_in_old),
        grid=(B, S // tq),
        in_specs=in_specs,
        out_specs=pl.BlockSpec((1, tq, H * d), lambda b, i: (b, i, 0)),
        out_shape=jax.ShapeDtypeStruct((B, S, H * d), BF16),
        scratch_shapes=[pltpu.VMEM((H, tq, d), F32), pltpu.VMEM((H, tq, LANES), F32)],
        compiler_params=_params(("parallel", "arbitrary")),
        name="attention",
    )(*args)


def _outproj_kernel(att_ref, c_ref, x_ref, wa_ref, wc_ref, g2_ref, x1_ref, xn_ref):
    x1 = (x_ref[...]
          + jnp.dot(att_ref[...], wa_ref[...], preferred_element_type=F32)
          + jnp.dot(c_ref[...], wc_ref[...], preferred_element_type=F32))
    x1_ref[...] = x1
    ms = jnp.mean(x1 * x1, axis=-1, keepdims=True)
    xn_ref[...] = ((x1 * lax.rsqrt(ms + EPS)) * g2_ref[...]).astype(BF16)


def _outproj(att, c, x, wa, wc, g2):
    N, D = x.shape
    T = min(N, ROW_TILE)
    assert N % T == 0
    row = lambda w: pl.BlockSpec((T, w), lambda i: (i, 0))
    return pl.pallas_call(
        _outproj_kernel,
        grid=(N // T,),
        in_specs=[row(att.shape[1]), row(c.shape[1]), row(D),
                  _const_spec(wa.shape), _const_spec(wc.shape), _const_spec(g2.shape)],
        out_specs=[row(D), row(D)],
        out_shape=[jax.ShapeDtypeStruct((N, D), F32), jax.ShapeDtypeStruct((N, D), BF16)],
        compiler_params=_params(("parallel",)),
        name="outproj",
    )(att, c, x, wa, wc, g2)


def _candidate_pairs():
    return [(p, q) for p in range(1, PEER_TOPK + 1) for q in range(1, PEER_TOPK + 1)
            if p * q <= PEER_TOPK]


def _sort_desc(v):
    v = list(v)
    n = len(v)
    k = 2
    while k <= n:
        j = k // 2
        while j >= 1:
            for i in range(n):
                m = i ^ j
                if m > i:
                    hi, lo = jnp.maximum(v[i], v[m]), jnp.minimum(v[i], v[m])
                    v[i], v[m] = (hi, lo) if (i & k) == 0 else (lo, hi)
            j //= 2
        k *= 2
    return v


def _merge_top(a, b):
    n = len(a)
    c = [jnp.maximum(a[i], b[n - 1 - i]) for i in range(n)]
    j = n // 2
    while j >= 1:
        for i in range(n):
            m = i ^ j
            if m > i:
                c[i], c[m] = jnp.maximum(c[i], c[m]), jnp.minimum(c[i], c[m])
        j //= 2
    return c


def _top_values(load, count):
    k = PEER_TOPK
    groups = []
    for g in range(0, count, k):
        vals = [load(n) for n in range(g, min(g + k, count))]
        vals += [jnp.full_like(vals[0], -jnp.inf)] * (k - len(vals))
        groups.append(_sort_desc(vals))
    while len(groups) > 1:
        merged = [_merge_top(groups[i], groups[i + 1]) for i in range(0, len(groups) - 1, 2)]
        groups = merged + ([groups[-1]] if len(groups) % 2 else [])
    return groups[0]


def _peer_route(xn_ref, wq_ref, kb_ref, xnt_ref, s_ref, top_ref, cand_ref, kth_ref,
                r0_ref, e0_ref, cnt_ref, e1_ref, *, nk, ph, tokens):
    xnt_ref[...] = xn_ref[...].astype(F32).T.astype(BF16)
    xnt = xnt_ref[...]
    for c in range(2):
        qt = jnp.dot(wq_ref[c], xnt, preferred_element_type=F32)
        sc = jnp.dot(kb_ref[c], qt.astype(BF16), preferred_element_type=F32)
        for lt in range(tokens // LANES):
            s_ref[c, lt] = sc[:, lt * LANES:(lt + 1) * LANES]

    pairs = _candidate_pairs()
    for lt in range(tokens // LANES):
        for c in range(2):
            top = _top_values(lambda n: s_ref[c, lt, n * ph:(n + 1) * ph, :], nk)
            for p in range(PEER_TOPK):
                top_ref[c, p, lt] = top[p]
        for n, (p, q) in enumerate(pairs):
            cand_ref[n, lt] = top_ref[0, p - 1, lt] + top_ref[1, q - 1, lt]
        kth_ref[lt] = _top_values(lambda n: cand_ref[n, lt], len(pairs))[PEER_TOPK - 1]
    theta = kth_ref[...]
    a1 = top_ref[0, 0]
    b1 = top_ref[1, 0]
    best = a1 + b1
    denom = None
    for n in range(len(pairs)):
        cv = cand_ref[n]
        t = jnp.where(cv >= theta, jnp.exp(cv - best), 0.0)
        denom = t if denom is None else denom + t
    inv = 1.0 / denom

    def first_half(i, _):
        rows = pl.ds(pl.multiple_of(i * ph, ph), ph)
        s0 = s_ref[0, :, rows, :]
        rank = jnp.ones_like(s0)
        for p in range(PEER_TOPK):
            rank = jnp.where(top_ref[0, p] > s0, p + 2.0, rank)
        r0_ref[:, rows, :] = _bf16_pair(rank)
        e0_ref[:, rows, :] = _bf16_pair(jnp.exp(s0 - a1) * inv)
        return 0
    lax.fori_loop(0, nk, first_half, 0)

    def second_half(j, _):
        rows = pl.ds(pl.multiple_of(j * ph, ph), ph)
        s1 = s_ref[1, :, rows, :]
        cnt = jnp.zeros_like(s1)
        for p in range(PEER_TOPK):
            cnt = jnp.where(top_ref[0, p] + s1 >= theta, p + 1.0, cnt)
        s_ref[1, :, rows, :] = cnt
        s_ref[0, :, rows, :] = jnp.exp(s1 - b1)
        return 0
    lax.fori_loop(0, nk, second_half, 0)
    for lt in range(tokens // LANES):
        for h in range(ph):
            cnt_ref[lt, h * nk:(h + 1) * nk, :] = (
                s_ref[1, lt, pl.ds(h, nk, stride=ph), :].astype(BF16))
            e1_ref[lt, h * nk:(h + 1) * nk, :] = (
                s_ref[0, lt, pl.ds(h, nk, stride=ph), :].astype(BF16))


def _bf16_pair(x):
    bits = pltpu.bitcast(x.astype(BF16).astype(F32), jnp.uint32)
    return bits | lax.shift_right_logical(bits, jnp.uint32(16))


def _zero_bits_after(x):
    bits = pltpu.bitcast(x, jnp.uint32)
    return lax.shift_right_logical(lax.shift_right_logical(bits, jnp.uint32(16)), jnp.uint32(16))


def _peer_gate_chunk(c, xu, coef_ref, r0_ref, e0_ref, cnt_ref, e1_ref, *, nk, ph, chunk, tokens):
    half_tile = (PEER_SUB // 2, LANES)

    for ib in range(chunk // nk):
        i = c * (chunk // nk) + ib
        hrows = pl.ds(pl.multiple_of(i * ph, ph), ph)
        row0 = ib * nk
        for lt in range(tokens // LANES):
            lanes = slice(lt * LANES, (lt + 1) * LANES)
            zero = _zero_bits_after(xu[row0:row0 + SUBLANES, lanes])
            rank = r0_ref[lt, hrows, :] | zero
            e0 = e0_ref[lt, hrows, :] | zero
            w = [None] * (nk // PEER_SUB)
            for h in range(ph):
                rank_b = pltpu.bitcast(jnp.broadcast_to(rank[h:h + 1], half_tile), BF16)
                e0_b = pltpu.bitcast(jnp.broadcast_to(e0[h:h + 1], half_tile), BF16)
                for jb in range(nk // PEER_SUB):
                    jrows = slice(h * nk + jb * PEER_SUB, h * nk + (jb + 1) * PEER_SUB)
                    t = jnp.where(rank_b <= cnt_ref[lt, jrows, :],
                                  e1_ref[lt, jrows, :], 0.0) * e0_b
                    w[jb] = t if w[jb] is None else w[jb] + t
            for jb in range(nk // PEER_SUB):
                erows = slice(row0 + jb * PEER_SUB, row0 + (jb + 1) * PEER_SUB)
                xs = xu[erows, lanes].astype(BF16)
                act = xs * (0.5 + 0.5 * lax.erf(xs * math.sqrt(0.5)))
                coef_ref[erows, lanes] = w[jb] * act


def _peer_kernel(xn_ref, x1_ref, wq_ref, kb_ref, u_ref, vt_ref, y_ref,
                 s_ref, top_ref, cand_ref, kth_ref, r0_ref, e0_ref, cnt_ref, e1_ref,
                 acc_ref, coef_ref, xnt_ref, *, nk, ph, chunk, tokens):
    e = pl.program_id(1)

    @pl.when(e == 0)
    def _():
        _peer_route(xn_ref, wq_ref, kb_ref, xnt_ref, s_ref, top_ref, cand_ref, kth_ref,
                    r0_ref, e0_ref, cnt_ref, e1_ref, nk=nk, ph=ph, tokens=tokens)
        acc_ref[...] = jnp.zeros_like(acc_ref)

    xu = jnp.dot(u_ref[...], xnt_ref[...], preferred_element_type=F32)
    _peer_gate_chunk(e, xu, coef_ref, r0_ref, e0_ref, cnt_ref, e1_ref,
                     nk=nk, ph=ph, chunk=chunk, tokens=tokens)
    acc_ref[...] += jnp.dot(vt_ref[...], coef_ref[...], preferred_element_type=F32)

    @pl.when(e == pl.num_programs(1) - 1)
    def _():
        y_ref[...] = x1_ref[...] + acc_ref[...].T


def _peer(xn, x1, wq_t, kbig, u_bf, vt_bf):
    N, D = xn.shape
    E = u_bf.shape[0]
    ph = PEER_HEADS
    nk = kbig.shape[1] // ph
    T = min(N, PEER_TOKENS)
    chunk = min(E, PEER_EXPERT_CHUNK)
    assert N % T == 0 and E % chunk == 0 and chunk % nk == 0 and T % LANES == 0
    npairs = len(_candidate_pairs())
    lt = T // LANES
    once = dict(pipeline_mode=pl.Buffered(1))
    return pl.pallas_call(
        functools.partial(_peer_kernel, nk=nk, ph=ph, chunk=chunk, tokens=T),
        grid=(N // T, E // chunk),
        in_specs=[pl.BlockSpec((T, D), lambda t, e: (t, 0)),
                  pl.BlockSpec((T, D), lambda t, e: (t, 0), **once),
                  pl.BlockSpec(wq_t.shape, lambda t, e: (0, 0, 0), **once),
                  pl.BlockSpec(kbig.shape, lambda t, e: (0, 0, 0), **once),
                  pl.BlockSpec((chunk, D), lambda t, e: (e, 0)),
                  pl.BlockSpec((D, chunk), lambda t, e: (0, e))],
        out_specs=pl.BlockSpec((T, D), lambda t, e: (t, 0)),
        out_shape=jax.ShapeDtypeStruct((N, D), F32),
        scratch_shapes=[pltpu.VMEM((2, lt, nk * ph, LANES), F32),
                        pltpu.VMEM((2, PEER_TOPK, lt, ph, LANES), F32),
                        pltpu.VMEM((npairs, lt, ph, LANES), F32),
                        pltpu.VMEM((lt, ph, LANES), F32),
                        pltpu.VMEM((lt, nk * ph, LANES), jnp.uint32),
                        pltpu.VMEM((lt, nk * ph, LANES), jnp.uint32),
                        pltpu.VMEM((lt, ph * nk, LANES), BF16),
                        pltpu.VMEM((lt, ph * nk, LANES), BF16),
                        pltpu.VMEM((D, T), F32),
                        pltpu.VMEM((chunk, T), BF16),
                        pltpu.VMEM((D, T), BF16)],
        compiler_params=_params(("parallel", "arbitrary")),
        name="peer",
    )(xn, x1, wq_t, kbig, u_bf, vt_bf)


def _layer_weights(norm1_g, w_in, q_norm_g, k_norm_g, conv_w, conv_b, conv_ln_g, conv_ln_b,
                   w_out, norm2_g, peer_wq, peer_keys, peer_u, peer_v):
    att_w = (w_in.shape[1] - 2 * conv_w.shape[1]) // 3
    heads = att_w // HEAD_DIM
    row = lambda v: v.reshape(1, -1)
    head_of = jnp.arange(att_w) // HEAD_DIM
    gm = jnp.where(head_of[:, None] == head_of[None, :], 1.0 / HEAD_DIM, 0.0).astype(BF16)
    _, nk, half = peer_keys.shape
    ph = PEER_HEADS
    D = peer_wq.shape[0]
    wq_t = peer_wq.reshape(D, ph, 2, half).transpose(2, 1, 3, 0).reshape(2, ph * half, D)
    eye = jnp.eye(ph, dtype=peer_keys.dtype)
    kbig = (peer_keys[:, :, None, None, :] * eye[None, None, :, :, None]
            ).reshape(2, nk * ph, ph * half)
    return dict(
        g1=row(norm1_g), w_in=w_in.astype(BF16),
        gq=row(jnp.tile(q_norm_g, heads)), gk=row(jnp.tile(k_norm_g, heads)), gm=gm,
        cw=conv_w, cb=row(conv_b), lg=row(conv_ln_g), lb=row(conv_ln_b),
        wa=w_out[:att_w].astype(BF16), wc=w_out[att_w:].astype(BF16), g2=row(norm2_g),
        wq_t=wq_t.astype(BF16), kbig=kbig.astype(BF16),
        u=peer_u.astype(BF16), vt=peer_v.T.astype(BF16))


def _layer(x, caches, conv_hist, w, layer):
    B, S, D = x.shape
    weights = (w["g1"], w["w_in"], w["gq"], w["gk"], w["gm"])
    if caches is None:
        q, kt, vt, kt_bf, v_bf, glu = _inproj(x, *weights, transposed=True)
        att = _attention(q, kt_bf, v_bf)
        k, v = jnp.swapaxes(kt, -1, -2), jnp.swapaxes(vt, -1, -2)
        conv_hist = jnp.zeros((B, w["cw"].shape[0] - 1, glu.shape[-1]), x.dtype)
    else:
        q, k, v, glu = _inproj(x, *weights, transposed=False)
        kt_new = jnp.swapaxes(k, -1, -2).astype(BF16)
        att = _attention(q, kt_new, v.astype(BF16), *caches, layer=layer)
    c, new_conv = _conv(glu, conv_hist, w["cw"], w["cb"], w["lg"], w["lb"])
    x1, xn = _outproj(att.reshape(B * S, -1), c.reshape(B * S, -1), x.reshape(B * S, D),
                      w["wa"], w["wc"], w["g2"])
    y = _peer(xn, x1, w["wq_t"], w["kbig"], w["u"], w["vt"])
    return y.reshape(B, S, D), k, v, new_conv


def kernel(x_prompt, x_sample, cache_k, cache_v, state_conv, norm1_g, w_in, q_norm_g, k_norm_g,
           conv_w, conv_b, conv_ln_g, conv_ln_b, w_out, norm2_g, peer_wq, peer_keys, peer_u,
           peer_v):
    y_p, y_s = x_prompt, x_sample
    cache_k = jnp.swapaxes(cache_k, -1, -2)
    cache_v = jnp.swapaxes(cache_v, -1, -2)
    outs = [[] for _ in range(6)]
    for l in range(norm1_g.shape[0]):
        w = _layer_weights(norm1_g[l], w_in[l], q_norm_g[l], k_norm_g[l], conv_w[l], conv_b[l],
                           conv_ln_g[l], conv_ln_b[l], w_out[l], norm2_g[l], peer_wq[l],
                           peer_keys[l], peer_u[l], peer_v[l])
        y_p, kp, vp, cp = _layer(y_p, None, None, w, l)
        y_s, ks, vs, cs = _layer(y_s, (cache_k, cache_v), state_conv[l], w, l)
        for lst, val in zip(outs, (kp, vp, cp, ks, vs, cs)):
            lst.append(val)
    return (y_p, y_s) + tuple(jnp.stack(o) if len(o) > 1 else o[0][None] for o in outs)
```

```python
import functools
import math

import jax
import jax.numpy as jnp
from jax import lax
from jax.experimental import pallas as pl
from jax.experimental.pallas import tpu as pltpu

F32 = jnp.float32
BF16 = jnp.bfloat16

EPS = 1e-6
HEAD_DIM = 64
PEER_HEADS = 8
PEER_TOPK = 16
LANES = 128
SUBLANES = 8
ROW_TILE = 1024
CONV_ROWS = 64
ATT_BLOCK = 256
PEER_TOKENS = 512
PEER_EXPERT_CHUNK = 2048
PEER_SUB = 16
VMEM_LIMIT = 60 * 1024 * 1024
DEAD_LOG = -120.0

_NT = (((1,), (1,)), ((), ()))


def _const_spec(shape):
    zeros = (0,) * len(shape)
    return pl.BlockSpec(shape, lambda *_: zeros)


def _params(semantics):
    return pltpu.CompilerParams(dimension_semantics=semantics, vmem_limit_bytes=VMEM_LIMIT)


def _inproj_kernel(x_ref, g1_ref, w_ref, gq_ref, gk_ref, gm_ref, q_ref, *out_refs,
                   heads, att_w, conv_ch, transposed):
    x = x_ref[0]
    ms = jnp.mean(x * x, axis=-1, keepdims=True)
    hb = ((x * lax.rsqrt(ms + EPS)) * g1_ref[...]).astype(BF16)

    def proj(c0, width):
        return jnp.dot(hb, w_ref[:, c0:c0 + width], preferred_element_type=F32)

    def head_norm(t, g):
        msq = jnp.dot((t * t).astype(BF16), gm_ref[...], preferred_element_type=F32)
        return (t * lax.rsqrt(msq + EPS)) * g

    qn = head_norm(proj(0, att_w), gq_ref[...]) * (HEAD_DIM ** -0.5)
    kn = head_norm(proj(att_w, att_w), gk_ref[...])
    v = proj(2 * att_w, att_w)
    if transposed:
        kt_ref, vt_ref, ktb_ref, vb_ref, glu_ref = out_refs
        knt, vt = kn.T, v.T
    else:
        k_ref, v_ref, glu_ref = out_refs
    for h in range(heads):
        sl = slice(h * HEAD_DIM, (h + 1) * HEAD_DIM)
        q_ref[0, h] = qn[:, sl].astype(BF16)
        if transposed:
            kt_ref[0, h] = knt[sl, :]
            vt_ref[0, h] = vt[sl, :]
            ktb_ref[0, h] = knt[sl, :].astype(BF16)
            vb_ref[0, h] = v[:, sl].astype(BF16)
        else:
            k_ref[0, h] = kn[:, sl]
            v_ref[0, h] = v[:, sl]
    a = proj(3 * att_w, conv_ch)
    gate = proj(3 * att_w + conv_ch, conv_ch)
    glu_ref[0] = a * jax.nn.sigmoid(gate)


def _inproj(x, g1, w_bf, gq, gk, gm, transposed):
    B, S, D = x.shape
    att_w = gq.shape[-1]
    heads = att_w // HEAD_DIM
    conv_ch = (w_bf.shape[1] - 3 * att_w) // 2
    T = min(S, ROW_TILE)
    assert S % T == 0
    hm = lambda dt: jax.ShapeDtypeStruct((B, heads, S, HEAD_DIM), dt)
    hspec = pl.BlockSpec((1, heads, T, HEAD_DIM), lambda b, s: (b, 0, s, 0))
    tm = lambda dt: jax.ShapeDtypeStruct((B, heads, HEAD_DIM, S), dt)
    tspec = pl.BlockSpec((1, heads, HEAD_DIM, T), lambda b, s: (b, 0, 0, s))
    if transposed:
        kv_specs, kv_shapes = [tspec, tspec, tspec, hspec], [tm(F32), tm(F32), tm(BF16), hm(BF16)]
    else:
        kv_specs, kv_shapes = [hspec, hspec], [hm(F32), hm(F32)]
    return pl.pallas_call(
        functools.partial(_inproj_kernel, heads=heads, att_w=att_w, conv_ch=conv_ch,
                          transposed=transposed),
        grid=(B, S // T),
        in_specs=[pl.BlockSpec((1, T, D), lambda b, s: (b, s, 0)),
                  _const_spec(g1.shape), _const_spec(w_bf.shape),
                  _const_spec(gq.shape), _const_spec(gk.shape), _const_spec(gm.shape)],
        out_specs=[hspec] + kv_specs + [pl.BlockSpec((1, T, conv_ch), lambda b, s: (b, s, 0))],
        out_shape=[hm(BF16)] + kv_shapes + [jax.ShapeDtypeStruct((B, S, conv_ch), F32)],
        compiler_params=_params(("parallel", "parallel")),
        name="inproj",
    )(x, g1, w_bf, gq, gk, gm)


def _conv_kernel(glu_ref, hist_ref, w_ref, b_ref, lg_ref, lb_ref, c_ref, new_ref, ext_ref,
                 *, rows, width, chunk):
    hist = width - 1
    base = 32
    off = base - hist
    s = pl.program_id(1)

    @pl.when(s == 0)
    def _():
        ext_ref[off:base, :] = hist_ref[0]
        ext_ref[base + rows:, :] = jnp.zeros((SUBLANES, ext_ref.shape[1]), F32)

    ext_ref[base:base + rows, :] = glu_ref[0]
    span = chunk + SUBLANES
    lanes_all = ext_ref.shape[1]
    for r0 in range(0, rows, chunk):
        cols = []
        for l0 in range(0, lanes_all, LANES):
            acc = None
            for b in range(SUBLANES):
                part = None
                for a in range((off + width - 1) // SUBLANES + 1):
                    w = SUBLANES * a + b - off
                    if 0 <= w < width:
                        term = (ext_ref[r0 + SUBLANES * a:r0 + SUBLANES * a + span, l0:l0 + LANES]
                                * w_ref[w:w + 1, l0:l0 + LANES])
                        part = term if part is None else part + term
                if part is not None:
                    acc = part[b:b + chunk] if acc is None else acc + part[b:b + chunk]
            cols.append(acc)
        acc = jnp.concatenate(cols, axis=1) + b_ref[...]
        mu = jnp.mean(acc, axis=-1, keepdims=True)
        d = acc - mu
        var = jnp.mean(d * d, axis=-1, keepdims=True)
        y = d * lax.rsqrt(var + EPS) * lg_ref[...] + lb_ref[...]
        c_ref[0, r0:r0 + chunk, :] = (y * jax.nn.sigmoid(y)).astype(BF16)
    tail = ext_ref[rows + off:rows + base, :]
    new_ref[0] = tail
    ext_ref[off:base, :] = tail


def _conv(glu, hist, cw, cb, lg, lb):
    B, S, C = glu.shape
    width = cw.shape[0]
    T = min(S, ROW_TILE)
    assert S % T == 0 and (T >= width - 1 or S == T)
    chunk = min(T, CONV_ROWS)
    return pl.pallas_call(
        functools.partial(_conv_kernel, rows=T, width=width, chunk=chunk),
        grid=(B, S // T),
        in_specs=[pl.BlockSpec((1, T, C), lambda b, s: (b, s, 0)),
                  pl.BlockSpec((1, width - 1, C), lambda b, s: (b, 0, 0)),
                  _const_spec(cw.shape), _const_spec(cb.shape),
                  _const_spec(lg.shape), _const_spec(lb.shape)],
        out_specs=[pl.BlockSpec((1, T, C), lambda b, s: (b, s, 0)),
                   pl.BlockSpec((1, width - 1, C), lambda b, s: (b, 0, 0))],
        out_shape=[jax.ShapeDtypeStruct((B, S, C), BF16),
                   jax.ShapeDtypeStruct((B, width - 1, C), F32)],
        scratch_shapes=[pltpu.VMEM((32 + T + SUBLANES, C), F32)],
        compiler_params=_params(("parallel", "arbitrary")),
        name="conv",
    )(glu, hist, cw, cb, lg, lb)


def _sb_block(qs, kts, vbs, carries, upper, ones, mask, v_transposed):
    tq, tk = qs[0].shape[0], kts[0].shape[1]
    zs = [jnp.dot(q, kt, preferred_element_type=F32) for q, kt in zip(qs, kts)]
    log_betas, terms = [], []
    for z in zs:
        sp = jnp.log1p(jnp.exp(-jnp.abs(z)))
        log_beta = jnp.minimum(z, 0.0) - sp
        log_1m = log_beta - z
        if mask is not None:
            log_1m = jnp.where(mask, log_1m, 0.0)
        hi = log_1m.astype(BF16)
        lo = (log_1m - hi.astype(F32)).astype(BF16)
        log_betas.append(log_beta)
        terms += [hi, lo]
    both = jnp.concatenate(terms, axis=0)
    if tk == LANES:
        p = jnp.dot(both, jnp.concatenate([upper, ones], axis=1), preferred_element_type=F32)
        after, total = p[:, :tk], p[:, tk:]
    else:
        after = jnp.dot(both, upper, preferred_element_type=F32)
        total = jnp.dot(both, ones, preferred_element_type=F32)
    sums = []
    for h in range(len(zs)):
        r = 2 * h * tq
        sums.append((after[r:r + tq] + after[r + tq:r + 2 * tq],
                     total[r:r + tq] + total[r + tq:r + 2 * tq]))
    outs, new_carries = [], []
    for log_beta, (after, total), carry, vb in zip(log_betas, sums, carries, vbs):
        logw = log_beta + after
        if carry is not None:
            logw = logw + (carry if tk == LANES else carry[:, :1])
        w = jnp.exp(logw)
        if mask is not None:
            w = jnp.where(mask, w, 0.0)
        if v_transposed:
            outs.append(lax.dot_general(w.astype(BF16), vb, _NT, preferred_element_type=F32))
        else:
            outs.append(jnp.dot(w.astype(BF16), vb, preferred_element_type=F32))
        new_carries.append(total if carry is None else carry + total)
    return outs, new_carries


def _max_all(xs):
    m = xs[0]
    for x in xs[1:]:
        m = jnp.maximum(m, x)
    return jnp.max(m)


def _attn_kernel(*refs, heads, tq, tk, old_blocks, diag_in_old):
    if diag_in_old:
        q_ref, ko_ref, vo_ref, o_ref, acc_ref, carry_ref = refs
    else:
        q_ref, kd_ref, vd_ref, ko_ref, vo_ref, o_ref, acc_ref, carry_ref = refs
    qi = pl.program_id(1)

    def tri(n):
        r = lax.broadcasted_iota(jnp.int32, (n, n), 0)
        c = lax.broadcasted_iota(jnp.int32, (n, n), 1)
        return r, c

    r, c = tri(tq)
    mask = c < r
    upper_d = jnp.where(r > c, 1.0, 0.0).astype(BF16)
    ones_d = jnp.ones((tq, LANES), BF16)
    hs = range(heads)
    qs = [q_ref[0, h] for h in hs]
    if diag_in_old:
        rows = pl.ds(pl.multiple_of(qi * tq, tq), tq)
        kts = [ko_ref[0, h, :, rows] for h in hs]
        vbs = [vo_ref[0, h, rows, :] for h in hs]
    else:
        kts = [kd_ref[0, h] for h in hs]
        vbs = [vd_ref[0, h] for h in hs]
    outs, carries = _sb_block(qs, kts, vbs, [None] * heads, upper_d, ones_d, mask, False)
    for h in hs:
        acc_ref[h] = outs[h]
        carry_ref[h] = carries[h]

    if old_blocks is None:
        j0 = qi - 1
    else:
        j0 = jnp.int32(old_blocks - 1)

    r, c = tri(tk)
    upper_o = jnp.where(r > c, 1.0, 0.0).astype(BF16)
    ones_o = jnp.ones((tk, LANES), BF16)

    def cond(state):
        j, live = state
        return jnp.logical_and(j >= 0, live > DEAD_LOG)

    def body(state):
        j, _ = state
        rows = pl.ds(pl.multiple_of(j * tk, tk), tk)
        qs = [q_ref[0, h] for h in hs]
        kts = [ko_ref[0, h, :, rows].astype(BF16) for h in hs]
        if diag_in_old:
            vbs = [vo_ref[0, h, rows, :] for h in hs]
        else:
            vbs = [vo_ref[0, h, :, rows].astype(BF16) for h in hs]
        old = [carry_ref[h] for h in hs]
        accs = [acc_ref[h] for h in hs]
        outs, new = _sb_block(qs, kts, vbs, old, upper_o, ones_o, None, not diag_in_old)
        for h in hs:
            acc_ref[h] = accs[h] + outs[h]
            carry_ref[h] = new[h]
        return j - 1, _max_all(new)

    lax.while_loop(cond, body, (j0, _max_all(carries)))
    for h in range(heads):
        o_ref[0, :, h * HEAD_DIM:(h + 1) * HEAD_DIM] = acc_ref[h].astype(BF16)


def _attention(q, kt_new, v_new, kt_old=None, vt_old=None, layer=0):
    B, H, S, d = q.shape
    if kt_old is None:
        tq = tk = min(S, ATT_BLOCK)
        assert S % tq == 0
        old_blocks, diag_in_old = None, True
        in_specs = [pl.BlockSpec((1, H, tq, d), lambda b, i: (b, 0, i, 0)),
                    pl.BlockSpec((1, H, d, S), lambda b, i: (b, 0, 0, 0)),
                    pl.BlockSpec((1, H, S, d), lambda b, i: (b, 0, 0, 0))]
        args = (q, kt_new, v_new)
    else:
        P = kt_old.shape[4]
        tq, tk = S, min(P, ATT_BLOCK)
        assert P % tk == 0
        old_blocks, diag_in_old = P // tk, False
        new = pl.BlockSpec((1, H, S, d), lambda b, i: (b, 0, 0, 0))
        new_t = pl.BlockSpec((1, H, d, S), lambda b, i: (b, 0, 0, 0))
        old = pl.BlockSpec((None, 1, H, d, P), lambda b, i: (layer, b, 0, 0, 0))
        in_specs = [new, new_t, new, old, old]
        args = (q, kt_new, v_new, kt_old, vt_old)
    return pl.pallas_call(
        functools.partial(_attn_kernel, heads=H, tq=tq, tk=tk, old_blocks=old_blocks,
                          diag_in_old=diag_in_old),
        grid=(B, S // tq),
        in_specs=in_specs,
        out_specs=pl.BlockSpec((1, tq, H * d), lambda b, i: (b, i, 0)),
        out_shape=jax.ShapeDtypeStruct((B, S, H * d), BF16),
        scratch_shapes=[pltpu.VMEM((H, tq, d), F32), pltpu.VMEM((H, tq, LANES), F32)],
        compiler_params=_params(("parallel", "arbitrary")),
        name="attention",
    )(*args)


def _outproj_kernel(att_ref, c_ref, x_ref, wa_ref, wc_ref, g2_ref, x1_ref, xn_ref):
    x1 = (x_ref[...]
          + jnp.dot(att_ref[...], wa_ref[...], preferred_element_type=F32)
          + jnp.dot(c_ref[...], wc_ref[...], preferred_element_type=F32))
    x1_ref[...] = x1
    ms = jnp.mean(x1 * x1, axis=-1, keepdims=True)
    xn_ref[...] = ((x1 * lax.rsqrt(ms + EPS)) * g2_ref[...]).astype(BF16)


def _outproj(att, c, x, wa, wc, g2):
    N, D = x.shape
    T = min(N, ROW_TILE)
    assert N % T == 0
    row = lambda w: pl.BlockSpec((T, w), lambda i: (i, 0))
    return pl.pallas_call(
        _outproj_kernel,
        grid=(N // T,),
        in_specs=[row(att.shape[1]), row(c.shape[1]), row(D),
                  _const_spec(wa.shape), _const_spec(wc.shape), _const_spec(g2.shape)],
        out_specs=[row(D), row(D)],
        out_shape=[jax.ShapeDtypeStruct((N, D), F32), jax.ShapeDtypeStruct((N, D), BF16)],
        compiler_params=_params(("parallel",)),
        name="outproj",
    )(att, c, x, wa, wc, g2)


def _candidate_pairs():
    return [(p, q) for p in range(1, PEER_TOPK + 1) for q in range(1, PEER_TOPK + 1)
            if p * q <= PEER_TOPK]


def _sort_desc(v):
    v = list(v)
    n = len(v)
    k = 2
    while k <= n:
        j = k // 2
        while j >= 1:
            for i in range(n):
                m = i ^ j
                if m > i:
                    hi, lo = jnp.maximum(v[i], v[m]), jnp.minimum(v[i], v[m])
                    v[i], v[m] = (hi, lo) if (i & k) == 0 else (lo, hi)
            j //= 2
        k *= 2
    return v


def _merge_top(a, b):
    n = len(a)
    c = [jnp.maximum(a[i], b[n - 1 - i]) for i in range(n)]
    j = n // 2
    while j >= 1:
        for i in range(n):
            m = i ^ j
            if m > i:
                c[i], c[m] = jnp.maximum(c[i], c[m]), jnp.minimum(c[i], c[m])
        j //= 2
    return c


def _top_values(load, count):
    k = PEER_TOPK
    groups = []
    for g in range(0, count, k):
        vals = [load(n) for n in range(g, min(g + k, count))]
        vals += [jnp.full_like(vals[0], -jnp.inf)] * (k - len(vals))
        groups.append(_sort_desc(vals))
    while len(groups) > 1:
        merged = [_merge_top(groups[i], groups[i + 1]) for i in range(0, len(groups) - 1, 2)]
        groups = merged + ([groups[-1]] if len(groups) % 2 else [])
    return groups[0]


def _peer_route(xn_ref, wq_ref, kb_ref, xnt_ref, s_ref, top_ref, cand_ref, kth_ref,
                r0_ref, e0_ref, cnt_ref, e1_ref, *, nk, ph, tokens):
    xnt_ref[...] = xn_ref[...].astype(F32).T.astype(BF16)
    xnt = xnt_ref[...]
    for c in range(2):
        qt = jnp.dot(wq_ref[c], xnt, preferred_element_type=F32)
        sc = jnp.dot(kb_ref[c], qt.astype(BF16), preferred_element_type=F32)
        for lt in range(tokens // LANES):
            s_ref[c, lt] = sc[:, lt * LANES:(lt + 1) * LANES]

    pairs = _candidate_pairs()
    for lt in range(tokens // LANES):
        for c in range(2):
            top = _top_values(lambda n: s_ref[c, lt, n * ph:(n + 1) * ph, :], nk)
            for p in range(PEER_TOPK):
                top_ref[c, p, lt] = top[p]
        for n, (p, q) in enumerate(pairs):
            cand_ref[n, lt] = top_ref[0, p - 1, lt] + top_ref[1, q - 1, lt]
        kth_ref[lt] = _top_values(lambda n: cand_ref[n, lt], len(pairs))[PEER_TOPK - 1]
    theta = kth_ref[...]
    a1 = top_ref[0, 0]
    b1 = top_ref[1, 0]
    best = a1 + b1
    denom = None
    for n in range(len(pairs)):
        cv = cand_ref[n]
        t = jnp.where(cv >= theta, jnp.exp(cv - best), 0.0)
        denom = t if denom is None else denom + t
    inv = 1.0 / denom

    def first_half(i, _):
        rows = pl.ds(pl.multiple_of(i * ph, ph), ph)
        s0 = s_ref[0, :, rows, :]
        rank = jnp.ones_like(s0)
        for p in range(PEER_TOPK):
            rank = jnp.where(top_ref[0, p] > s0, p + 2.0, rank)
        r0_ref[:, rows, :] = _bf16_pair(rank)
        e0_ref[:, rows, :] = _bf16_pair(jnp.exp(s0 - a1) * inv)
        return 0
    lax.fori_loop(0, nk, first_half, 0)

    def second_half(j, _):
        rows = pl.ds(pl.multiple_of(j * ph, ph), ph)
        s1 = s_ref[1, :, rows, :]
        cnt = jnp.zeros_like(s1)
        for p in range(PEER_TOPK):
            cnt = jnp.where(top_ref[0, p] + s1 >= theta, p + 1.0, cnt)
        s_ref[1, :, rows, :] = cnt
        s_ref[0, :, rows, :] = jnp.exp(s1 - b1)
        return 0
    lax.fori_loop(0, nk, second_half, 0)
    for lt in range(tokens // LANES):
        for h in range(ph):
            cnt_ref[lt, h * nk:(h + 1) * nk, :] = (
                s_ref[1, lt, pl.ds(h, nk, stride=ph), :].astype(BF16))
            e1_ref[lt, h * nk:(h + 1) * nk, :] = (
                s_ref[0, lt, pl.ds(h, nk, stride=ph), :].astype(BF16))


def _bf16_pair(x):
    bits = pltpu.bitcast(x.astype(BF16).astype(F32), jnp.uint32)
    return bits | lax.shift_right_logical(bits, jnp.uint32(16))


def _zero_bits_after(x):
    bits = pltpu.bitcast(x, jnp.uint32)
    return lax.shift_right_logical(lax.shift_right_logical(bits, jnp.uint32(16)), jnp.uint32(16))


def _peer_gate_chunk(c, xu, coef_ref, r0_ref, e0_ref, cnt_ref, e1_ref, *, nk, ph, chunk, tokens):
    half_tile = (PEER_SUB // 2, LANES)

    for ib in range(chunk // nk):
        i = c * (chunk // nk) + ib
        hrows = pl.ds(pl.multiple_of(i * ph, ph), ph)
        row0 = ib * nk
        for lt in range(tokens // LANES):
            lanes = slice(lt * LANES, (lt + 1) * LANES)
            zero = _zero_bits_after(xu[row0:row0 + SUBLANES, lanes])
            rank = r0_ref[lt, hrows, :] | zero
            e0 = e0_ref[lt, hrows, :] | zero
            w = [None] * (nk // PEER_SUB)
            for h in range(ph):
                if h:
                    zero = _zero_bits_after(pltpu.bitcast(w[-1], jnp.uint32))
                rank_b = pltpu.bitcast(jnp.broadcast_to(rank[h:h + 1], half_tile) | zero, BF16)
                e0_b = pltpu.bitcast(jnp.broadcast_to(e0[h:h + 1], half_tile) | zero, BF16)
                for jb in range(nk // PEER_SUB):
                    jrows = slice(h * nk + jb * PEER_SUB, h * nk + (jb + 1) * PEER_SUB)
                    t = jnp.where(rank_b <= cnt_ref[lt, jrows, :],
                                  e1_ref[lt, jrows, :], 0.0) * e0_b
                    w[jb] = t if w[jb] is None else w[jb] + t
            for jb in range(nk // PEER_SUB):
                erows = slice(row0 + jb * PEER_SUB, row0 + (jb + 1) * PEER_SUB)
                xs = xu[erows, lanes].astype(BF16)
                act = xs * (0.5 + 0.5 * lax.erf(xs * math.sqrt(0.5)))
                coef_ref[erows, lanes] = w[jb] * act


def _peer_kernel(xn_ref, x1_ref, wq_ref, kb_ref, u_ref, vt_ref, y_ref,
                 s_ref, top_ref, cand_ref, kth_ref, r0_ref, e0_ref, cnt_ref, e1_ref,
                 acc_ref, coef_ref, xnt_ref, *, nk, ph, chunk, tokens):
    e = pl.program_id(1)

    @pl.when(e == 0)
    def _():
        _peer_route(xn_ref, wq_ref, kb_ref, xnt_ref, s_ref, top_ref, cand_ref, kth_ref,
                    r0_ref, e0_ref, cnt_ref, e1_ref, nk=nk, ph=ph, tokens=tokens)
        acc_ref[...] = jnp.zeros_like(acc_ref)

    xu = jnp.dot(u_ref[...], xnt_ref[...], preferred_element_type=F32)
    _peer_gate_chunk(e, xu, coef_ref, r0_ref, e0_ref, cnt_ref, e1_ref,
                     nk=nk, ph=ph, chunk=chunk, tokens=tokens)
    acc_ref[...] += jnp.dot(vt_ref[...], coef_ref[...], preferred_element_type=F32)

    @pl.when(e == pl.num_programs(1) - 1)
    def _():
        y_ref[...] = x1_ref[...] + acc_ref[...].T


def _peer(xn, x1, wq_t, kbig, u_bf, vt_bf):
    N, D = xn.shape
    E = u_bf.shape[0]
    ph = PEER_HEADS
    nk = kbig.shape[1] // ph
    T = min(N, PEER_TOKENS)
    chunk = min(E, PEER_EXPERT_CHUNK)
    assert N % T == 0 and E % chunk == 0 and chunk % nk == 0 and T % LANES == 0
    npairs = len(_candidate_pairs())
    lt = T // LANES
    once = dict(pipeline_mode=pl.Buffered(1))
    return pl.pallas_call(
        functools.partial(_peer_kernel, nk=nk, ph=ph, chunk=chunk, tokens=T),
        grid=(N // T, E // chunk),
        in_specs=[pl.BlockSpec((T, D), lambda t, e: (t, 0)),
                  pl.BlockSpec((T, D), lambda t, e: (t, 0), **once),
                  pl.BlockSpec(wq_t.shape, lambda t, e: (0, 0, 0), **once),
                  pl.BlockSpec(kbig.shape, lambda t, e: (0, 0, 0), **once),
                  pl.BlockSpec((chunk, D), lambda t, e: (e, 0)),
                  pl.BlockSpec((D, chunk), lambda t, e: (0, e))],
        out_specs=pl.BlockSpec((T, D), lambda t, e: (t, 0)),
        out_shape=jax.ShapeDtypeStruct((N, D), F32),
        scratch_shapes=[pltpu.VMEM((2, lt, nk * ph, LANES), F32),
                        pltpu.VMEM((2, PEER_TOPK, lt, ph, LANES), F32),
                        pltpu.VMEM((npairs, lt, ph, LANES), F32),
                        pltpu.VMEM((lt, ph, LANES), F32),
                        pltpu.VMEM((lt, nk * ph, LANES), jnp.uint32),
                        pltpu.VMEM((lt, nk * ph, LANES), jnp.uint32),
                        pltpu.VMEM((lt, ph * nk, LANES), BF16),
                        pltpu.VMEM((lt, ph * nk, LANES), BF16),
                        pltpu.VMEM((D, T), F32),
                        pltpu.VMEM((chunk, T), BF16),
                        pltpu.VMEM((D, T), BF16)],
        compiler_params=_params(("parallel", "arbitrary")),
        name="peer",
    )(xn, x1, wq_t, kbig, u_bf, vt_bf)


def _layer_weights(norm1_g, w_in, q_norm_g, k_norm_g, conv_w, conv_b, conv_ln_g, conv_ln_b,
                   w_out, norm2_g, peer_wq, peer_keys, peer_u, peer_v):
    att_w = (w_in.shape[1] - 2 * conv_w.shape[1]) // 3
    heads = att_w // HEAD_DIM
    row = lambda v: v.reshape(1, -1)
    head_of = jnp.arange(att_w) // HEAD_DIM
    gm = jnp.where(head_of[:, None] == head_of[None, :], 1.0 / HEAD_DIM, 0.0).astype(BF16)
    _, nk, half = peer_keys.shape
    ph = PEER_HEADS
    D = peer_wq.shape[0]
    wq_t = peer_wq.reshape(D, ph, 2, half).transpose(2, 1, 3, 0).reshape(2, ph * half, D)
    eye = jnp.eye(ph, dtype=peer_keys.dtype)
    kbig = (peer_keys[:, :, None, None, :] * eye[None, None, :, :, None]
            ).reshape(2, nk * ph, ph * half)
    return dict(
        g1=row(norm1_g), w_in=w_in.astype(BF16),
        gq=row(jnp.tile(q_norm_g, heads)), gk=row(jnp.tile(k_norm_g, heads)), gm=gm,
        cw=conv_w, cb=row(conv_b), lg=row(conv_ln_g), lb=row(conv_ln_b),
        wa=w_out[:att_w].astype(BF16), wc=w_out[att_w:].astype(BF16), g2=row(norm2_g),
        wq_t=wq_t.astype(BF16), kbig=kbig.astype(BF16),
        u=peer_u.astype(BF16), vt=peer_v.T.astype(BF16))


def _layer(x, caches, conv_hist, w, layer):
    B, S, D = x.shape
    weights = (w["g1"], w["w_in"], w["gq"], w["gk"], w["gm"])
    if caches is None:
        q, kt, vt, kt_bf, v_bf, glu = _inproj(x, *weights, transposed=True)
        att = _attention(q, kt_bf, v_bf)
        k, v = jnp.swapaxes(kt, -1, -2), jnp.swapaxes(vt, -1, -2)
        conv_hist = jnp.zeros((B, w["cw"].shape[0] - 1, glu.shape[-1]), x.dtype)
    else:
        q, k, v, glu = _inproj(x, *weights, transposed=False)
        kt_new = jnp.swapaxes(k, -1, -2).astype(BF16)
        att = _attention(q, kt_new, v.astype(BF16), *caches, layer=layer)
    c, new_conv = _conv(glu, conv_hist, w["cw"], w["cb"], w["lg"], w["lb"])
    x1, xn = _outproj(att.reshape(B * S, -1), c.reshape(B * S, -1), x.reshape(B * S, D),
                      w["wa"], w["wc"], w["g2"])
    y = _peer(xn, x1, w["wq_t"], w["kbig"], w["u"], w["vt"])
    return y.reshape(B, S, D), k, v, new_conv


def kernel(x_prompt, x_sample, cache_k, cache_v, state_conv, norm1_g, w_in, q_norm_g, k_norm_g,
           conv_w, conv_b, conv_ln_g, conv_ln_b, w_out, norm2_g, peer_wq, peer_keys, peer_u,
           peer_v):
    y_p, y_s = x_prompt, x_sample
    cache_k = jnp.swapaxes(cache_k, -1, -2)
    cache_v = jnp.swapaxes(cache_v, -1, -2)
    outs = [[] for _ in range(6)]
    for l in range(norm1_g.shape[0]):
        w = _layer_weights(norm1_g[l], w_in[l], q_norm_g[l], k_norm_g[l], conv_w[l], conv_b[l],
                           conv_ln_g[l], conv_ln_b[l], w_out[l], norm2_g[l], peer_wq[l],
                           peer_keys[l], peer_u[l], peer_v[l])
        y_p, kp, vp, cp = _layer(y_p, None, None, w, l)
        y_s, ks, vs, cs = _layer(y_s, (cache_k, cache_v), state_conv[l], w, l)
        for lst, val in zip(outs, (kp, vp, cp, ks, vs, cs)):
            lst.append(val)
    return (y_p, y_s) + tuple(jnp.stack(o) if len(o) > 1 else o[0][None] for o in outs)
```
